```python
import math
import jax, jax.numpy as jnp
from jax import lax
import numpy as np

D_MODEL = 1024
BATCH = 4
SEQ = 4096
DEPTH = 1

SSD_EXPAND = 2
D_INNER = SSD_EXPAND * D_MODEL
SSD_HEAD_DIM = 64
SSD_HEADS = D_INNER // SSD_HEAD_DIM
SSD_GROUPS = 8
D_STATE = 128
CONV_K = 4
CONV_DIM = D_INNER + 2 * SSD_GROUPS * D_STATE
SSD_CHUNK = 128
SB_HEADS = 16
SB_HEAD_DIM = 64
SB_WIDTH = SB_HEADS * SB_HEAD_DIM
SB_BLOCK = 128
D_FF = 4 * D_MODEL
PLE_DIM = 256
N_BRANCHES = 2
RMS_EPS = 1e-6
IN_PROJ_DIM = D_INNER + CONV_DIM + SSD_HEADS + 3 * SB_WIDTH
SPLITS = [D_INNER, D_INNER + CONV_DIM, D_INNER + CONV_DIM + SSD_HEADS,
          D_INNER + CONV_DIM + SSD_HEADS + SB_WIDTH,
          D_INNER + CONV_DIM + SSD_HEADS + 2 * SB_WIDTH]

kernel_name = "hybrid_ssd_stickbreak_gated_block"


def rms_norm(x, w):
    xf = x.astype(jnp.float32)
    y = xf * lax.rsqrt(jnp.mean(xf * xf, axis=-1, keepdims=True) + RMS_EPS)
    return (y * w.astype(jnp.float32)).astype(x.dtype)


def causal_depthwise_conv(u, w, b):
    k = w.shape[0]
    c = u.shape[-1]
    y = lax.conv_general_dilated(u, w[:, None, :].astype(u.dtype), window_strides=(1,),
                                 padding=[(k - 1, 0)],
                                 dimension_numbers=("NWC", "WIO", "NWC"),
                                 feature_group_count=c)
    return y + b.astype(u.dtype)


def ssd_chunked_scan(xs, dt, a, bmat, cmat):
    bsz, seqlen, nh, hd = xs.shape
    ng, ns = bmat.shape[2], bmat.shape[3]
    r = nh // ng
    nc = seqlen // SSD_CHUNK
    shp = (bsz, nc, SSD_CHUNK, ng, r)
    xd = (xs * dt[..., None]).reshape(shp + (hd,))
    a_cs = jnp.cumsum((dt * a).reshape(shp), axis=2)
    bc = bmat.reshape(bsz, nc, SSD_CHUNK, ng, ns)
    cc = cmat.reshape(bsz, nc, SSD_CHUNK, ng, ns)
    seg = a_cs[:, :, :, None] - a_cs[:, :, None, :]
    causal = jnp.tril(jnp.ones((SSD_CHUNK, SSD_CHUNK), dtype=bool))[None, None, :, :, None, None]
    decay = jnp.exp(jnp.where(causal, seg, -jnp.inf))
    cb = jnp.einsum("bclgn,bcsgn->bclsg", cc, bc)
    y_diag = jnp.einsum("bclsgr,bcsgrp->bclgrp", cb[..., None] * decay, xd)
    decay_to_end = jnp.exp(a_cs[:, :, -1:] - a_cs)
    states = jnp.einsum("bcsgn,bcsgrp->bcgrpn", bc, xd * decay_to_end[..., None])
    chunk_decay = jnp.exp(a_cs[:, :, -1])

    def step(carry, inp):
        st, dec = inp
        return carry * dec[..., None, None] + st, carry

    init = jnp.zeros((bsz, ng, r, hd, ns), jnp.float32)
    _, prev = lax.scan(step, init, (jnp.moveaxis(states, 1, 0), jnp.moveaxis(chunk_decay, 1, 0)))
    prev = jnp.moveaxis(prev, 0, 1)
    y_off = jnp.einsum("bclgn,bcgrpn->bclgrp", cc, prev) * jnp.exp(a_cs)[..., None]
    return (y_diag + y_off).reshape(bsz, seqlen, nh, hd)


def ssd_mixer(z, xbc, dt_raw, conv_w, conv_b, dt_bias, a_log, d_skip, norm_w):
    bsz, seqlen, _ = z.shape
    xbc = jax.nn.silu(causal_depthwise_conv(xbc, conv_w, conv_b))
    xs, bmat, cmat = jnp.split(xbc, [D_INNER, D_INNER + SSD_GROUPS * D_STATE], axis=-1)
    xs = xs.astype(jnp.float32).reshape(bsz, seqlen, SSD_HEADS, SSD_HEAD_DIM)
    bmat = bmat.astype(jnp.float32).reshape(bsz, seqlen, SSD_GROUPS, D_STATE)
    cmat = cmat.astype(jnp.float32).reshape(bsz, seqlen, SSD_GROUPS, D_STATE)
    dt = jax.nn.softplus(dt_raw.astype(jnp.float32) + dt_bias.astype(jnp.float32))
    a = -jnp.exp(a_log.astype(jnp.float32))
    y = ssd_chunked_scan(xs, dt, a, bmat, cmat) + xs * d_skip.astype(jnp.float32)[:, None]
    y = y.reshape(bsz, seqlen, D_INNER) * jax.nn.silu(z.astype(jnp.float32))
    yg = y.reshape(bsz, seqlen, SSD_GROUPS, D_INNER // SSD_GROUPS)
    yg = yg * lax.rsqrt(jnp.mean(yg * yg, axis=-1, keepdims=True) + RMS_EPS)
    y = yg.reshape(bsz, seqlen, D_INNER) * norm_w.astype(jnp.float32)
    return y.astype(z.dtype)


def stick_breaking_attention(q, k, v):
    seqlen, hd = q.shape[2], q.shape[3]
    scale = hd ** -0.5
    outs = []
    for blk in range(seqlen // SB_BLOCK):
        t0 = blk * SB_BLOCK
        t1 = t0 + SB_BLOCK
        logits = jnp.einsum("bhtd,bhsd->bhts", q[:, :, t0:t1], k[:, :, :t1]).astype(jnp.float32) * scale
        mask = jnp.arange(t1)[None, :] < jnp.arange(t0, t1)[:, None]
        log_keep = jnp.where(mask, jax.nn.log_sigmoid(-logits), 0.0)
        later = jnp.flip(jnp.cumsum(jnp.flip(log_keep, -1), axis=-1), -1) - log_keep
        weights = jnp.where(mask, jnp.exp(jax.nn.log_sigmoid(logits) + later), 0.0)
        outs.append(jnp.einsum("bhts,bhsd->bhtd", weights, v[:, :, :t1].astype(jnp.float32)))
    return jnp.concatenate(outs, axis=2)


def setup_inputs(seed: int = 0) -> dict:
    key = jax.random.key(seed)
    ks = jax.random.split(key, 26)
    f32 = jnp.float32

    def nrm(k, shape, fan_in):
        return jax.random.normal(k, shape, f32) * (fan_in ** -0.5)

    def gain(k, shape):
        return 1.0 + 0.05 * jax.random.normal(k, shape, f32)

    dt0 = jnp.exp(jax.random.uniform(ks[6], (DEPTH, SSD_HEADS), f32)
                  * (math.log(0.1) - math.log(0.001)) + math.log(0.001))
    dt_bias = dt0 + jnp.log(-jnp.expm1(-dt0))
    a_log = jnp.log(jax.random.uniform(ks[7], (DEPTH, SSD_HEADS), f32, 1.0, 16.0))
    return {
        "x": jax.random.normal(ks[0], (BATCH, SEQ, D_MODEL), f32),
        "p": jax.random.normal(ks[1], (DEPTH, BATCH, SEQ, PLE_DIM), f32),
        "norm_mix_pre": gain(ks[2], (DEPTH, D_MODEL)),
        "w_in": nrm(ks[3], (DEPTH, D_MODEL, IN_PROJ_DIM), D_MODEL),
        "conv_w": nrm(ks[4], (DEPTH, CONV_K, CONV_DIM), CONV_K),
        "conv_b": 0.01 * jax.random.normal(ks[5], (DEPTH, CONV_DIM), f32),
        "dt_bias": dt_bias,
        "a_log": a_log,
        "d_skip": 1.0 + 0.1 * jax.random.normal(ks[8], (DEPTH, SSD_HEADS), f32),
        "ssd_norm": gain(ks[9], (DEPTH, D_INNER)),
        "w_ssd_branch": nrm(ks[10], (DEPTH, D_INNER, D_MODEL), D_INNER),
        "w_sb_branch": nrm(ks[11], (DEPTH, SB_WIDTH, D_MODEL), SB_WIDTH),
        "w_gate": nrm(ks[12], (DEPTH, D_MODEL, N_BRANCHES * D_MODEL), D_MODEL),
        "b_gate": 0.01 * jax.random.normal(ks[13], (DEPTH, N_BRANCHES * D_MODEL), f32),
        "w_out": nrm(ks[14], (DEPTH, D_MODEL, D_MODEL), D_MODEL),
        "norm_mix_post": gain(ks[15], (DEPTH, D_MODEL)),
        "norm_ffn_pre": gain(ks[16], (DEPTH, D_MODEL)),
        "w_ff1": nrm(ks[17], (DEPTH, D_MODEL, D_FF), D_MODEL),
        "w_ff2": nrm(ks[18], (DEPTH, D_FF, D_MODEL), D_FF),
        "norm_ffn_post": gain(ks[19], (DEPTH, D_MODEL)),
        "w_ple": nrm(ks[20], (DEPTH, PLE_DIM, D_MODEL), PLE_DIM),
        "w_ple_gate": nrm(ks[21], (DEPTH, D_MODEL, D_MODEL), D_MODEL),
        "norm_ple_post": gain(ks[22], (DEPTH, D_MODEL)),
    }


def reference(x, p, norm_mix_pre, w_in, conv_w, conv_b, dt_bias, a_log, d_skip, ssd_norm,
              w_ssd_branch, w_sb_branch, w_gate, b_gate, w_out, norm_mix_post,
              norm_ffn_pre, w_ff1, w_ff2, norm_ffn_post, w_ple, w_ple_gate, norm_ple_post):
    h = x
    bsz, seqlen, _ = x.shape
    for i in range(DEPTH):
        n1 = rms_norm(h, norm_mix_pre[i])
        proj = n1 @ w_in[i]
        z, xbc, dt_raw, q, k, v = jnp.split(proj, SPLITS, axis=-1)
        y_ssd = ssd_mixer(z, xbc, dt_raw, conv_w[i], conv_b[i], dt_bias[i], a_log[i],
                          d_skip[i], ssd_norm[i])
        heads = lambda t: t.reshape(bsz, seqlen, SB_HEADS, SB_HEAD_DIM).transpose(0, 2, 1, 3)
        y_sb = stick_breaking_attention(heads(q), heads(k), heads(v))
        y_sb = y_sb.transpose(0, 2, 1, 3).reshape(bsz, seqlen, SB_WIDTH).astype(h.dtype)
        gates = jax.nn.sigmoid((n1 @ w_gate[i] + b_gate[i]).astype(jnp.float32)).astype(h.dtype)
        g_ssd, g_sb = jnp.split(gates, N_BRANCHES, axis=-1)
        merged = g_ssd * (y_ssd @ w_ssd_branch[i]) + g_sb * (y_sb @ w_sb_branch[i])
        h = h + rms_norm(merged @ w_out[i], norm_mix_post[i])
        n2 = rms_norm(h, norm_ffn_pre[i])
        ff = jnp.square(jax.nn.relu(n2 @ w_ff1[i])) @ w_ff2[i]
        h = h + rms_norm(ff, norm_ffn_post[i])
        ple_gate = jax.nn.sigmoid((h @ w_ple_gate[i]).astype(jnp.float32)).astype(h.dtype)
        h = h + rms_norm(ple_gate * (p[i].astype(h.dtype) @ w_ple[i]), norm_ple_post[i])
    return h
```

```python
import functools
import math

import jax
import jax.numpy as jnp
from jax import lax
from jax.experimental import pallas as pl
from jax.experimental.pallas import tpu as pltpu

F32 = jnp.float32
BF16 = jnp.bfloat16

RMS_EPS = 1e-6
LOG2E = math.log2(math.e)

D_MODEL = 1024
D_INNER = 2048
SSD_HEAD_DIM = 64
SSD_HEADS = 32
SSD_GROUPS = 8
HEADS_PER_GROUP = SSD_HEADS // SSD_GROUPS
D_STATE = 128
CONV_K = 4
SSD_CHUNK = 128
SB_HEADS = 16
SB_HEAD_DIM = 64
SB_WIDTH = 1024
D_FF = 4096
PLE_DIM = 256

LANES = 128
SUBLANES = 8
VMEM_LIMIT_BYTES = 56 * 1024 * 1024

PROJ_TM = 1024
PROJ_TN = 768
ATTN_TQ = 256
ATTN_TK = 256
MERGE_TM = 512
FFN_TM = 1024
FFN_TF = 1024

DT_PAD = 256
PROJ_F32_COLS = D_INNER + (D_INNER + 2 * SSD_GROUPS * D_STATE) + 2 * D_MODEL + DT_PAD
COL_Z = 0
COL_X = D_INNER
COL_BC = 2 * D_INNER
COL_GATE = 3 * D_INNER
COL_DT = 3 * D_INNER + 2 * D_MODEL


def _compiler_params(semantics):
    return pltpu.CompilerParams(dimension_semantics=semantics, vmem_limit_bytes=VMEM_LIMIT_BYTES)


def _rms_scale(x):
    return lax.rsqrt(jnp.mean(x * x, axis=-1, keepdims=True) + RMS_EPS)


def _sigmoid(x):
    return 1.0 / (1.0 + jnp.exp(-x))


def _softplus(x):
    return jnp.maximum(x, 0.0) + jnp.log(1.0 + jnp.exp(-jnp.abs(x)))


def _norm_matmul_kernel(x_ref, g_ref, w_ref, o_ref, n_ref):
    @pl.when(pl.program_id(1) == 0)
    def _():
        x = x_ref[...]
        n_ref[...] = (x * _rms_scale(x) * g_ref[...]).astype(BF16)

    o_ref[...] = jnp.dot(n_ref[...], w_ref[...], preferred_element_type=F32).astype(o_ref.dtype)


def _norm_matmul(x, g, w, out_dtype, tm, tn):
    t, d = x.shape
    n = w.shape[1]
    return pl.pallas_call(
        _norm_matmul_kernel,
        grid=(t // tm, n // tn),
        in_specs=[
            pl.BlockSpec((tm, d), lambda i, j: (i, 0)),
            pl.BlockSpec((1, d), lambda i, j: (0, 0)),
            pl.BlockSpec((d, tn), lambda i, j: (0, j)),
        ],
        out_specs=pl.BlockSpec((tm, tn), lambda i, j: (i, j)),
        out_shape=jax.ShapeDtypeStruct((t, n), out_dtype),
        scratch_shapes=[pltpu.VMEM((tm, d), BF16)],
        compiler_params=_compiler_params(("parallel", "arbitrary")),
        name="norm_matmul",
    )(x, g, w)


def _split3(x):
    p1 = x.astype(BF16)
    r1 = x - p1.astype(F32)
    p2 = r1.astype(BF16)
    r2 = r1 - p2.astype(F32)
    return p1, p2, r2.astype(BF16)


def _ssd_kernel(z_ref, xs_ref, bc_ref, dt_ref, cwx_ref, cwbc_ref, cbx_ref, cbbc_ref,
                dtb_ref, alog_ref, dskip_ref, normw_ref, o_ref,
                xin_ref, bcin_ref, state_ref):
    L = SSD_CHUNK
    c = pl.program_id(1)

    @pl.when(c == 0)
    def _():
        xin_ref[0:SUBLANES, :] = jnp.zeros((SUBLANES, D_INNER), F32)
        bcin_ref[0:SUBLANES, :] = jnp.zeros((SUBLANES, D_INNER), F32)
        state_ref[...] = jnp.zeros_like(state_ref)

    xin_ref[SUBLANES:SUBLANES + L, :] = xs_ref[...]
    bcin_ref[SUBLANES:SUBLANES + L, :] = bc_ref[...]

    dt = _softplus(dt_ref[...] + dtb_ref[...])
    a = -jnp.exp(alog_ref[...])
    row = lax.broadcasted_iota(jnp.int32, (L, L), 0)
    col = lax.broadcasted_iota(jnp.int32, (L, L), 1)
    causal = row >= col
    tril = jnp.where(causal, 1.0, 0.0).astype(BF16)
    p1, p2, p3 = _split3(dt * a)
    a_cs = (jnp.dot(tril, p1, preferred_element_type=F32)
            + jnp.dot(tril, p2, preferred_element_type=F32)
            + jnp.dot(tril, p3, preferred_element_type=F32))
    a_tot = a_cs[L - 1:L, :]
    w_end = dt * jnp.exp(a_tot - a_cs)
    a_cs_t = a_cs.T
    dt_t = dt.T
    w_end_t = w_end.T
    first_half = lax.broadcasted_iota(jnp.int32, (L, LANES), 1) < SSD_HEAD_DIM

    def conv_silu(in_ref, w_ref, b_ref, lo, width):
        acc = b_ref[:, lo:lo + width]
        for k in range(CONV_K):
            off = SUBLANES - (CONV_K - 1) + k
            acc = acc + w_ref[k:k + 1, lo:lo + width] * in_ref[off:off + L, lo:lo + width]
        return acc * _sigmoid(acc)

    def col_bcast(mat_t, h):
        return jnp.broadcast_to(mat_t[h:h + 1, :], (L, L)).T

    for g in range(SSD_GROUPS):
        xg = conv_silu(xin_ref, cwx_ref, cbx_ref, g * 256, 256)
        bg = conv_silu(bcin_ref, cwbc_ref, cbbc_ref, g * D_STATE, D_STATE)
        cg = conv_silu(bcin_ref, cwbc_ref, cbbc_ref, SSD_GROUPS * D_STATE + g * D_STATE, D_STATE)
        xg_b = xg.astype(BF16)
        bg_b = bg.astype(BF16)
        cg_b = cg.astype(BF16)
        cb = lax.dot_general(cg_b, bg_b, (((1,), (1,)), ((), ())), preferred_element_type=F32)
        st = state_ref[g]
        y_off = jnp.dot(cg_b, st.astype(BF16), preferred_element_type=F32)

        y_diag = []
        ea_cols = []
        w_cols = []
        for j in range(HEADS_PER_GROUP):
            h = g * HEADS_PER_GROUP + j
            a_col = col_bcast(a_cs_t, h)
            seg = a_col - a_cs_t[h:h + 1, :]
            decay = jnp.exp(jnp.where(causal, seg, -jnp.inf))
            m = (cb * decay * dt_t[h:h + 1, :]).astype(BF16)
            pair = (j // 2) * LANES
            y_diag.append(jnp.dot(m, xg_b[:, pair:pair + LANES], preferred_element_type=F32))
            ea_cols.append(jnp.exp(a_col))
            w_cols.append(col_bcast(w_end_t, h))
        yd = jnp.concatenate([jnp.where(first_half, y_diag[0], y_diag[1]),
                              jnp.where(first_half, y_diag[2], y_diag[3])], axis=1)
        ea = jnp.concatenate([jnp.where(first_half, ea_cols[0], ea_cols[1]),
                              jnp.where(first_half, ea_cols[2], ea_cols[3])], axis=1)
        wf = jnp.concatenate([jnp.where(first_half, w_cols[0], w_cols[1]),
                              jnp.where(first_half, w_cols[2], w_cols[3])], axis=1)
        y = yd + y_off * ea + xg * dskip_ref[:, g * 256:(g + 1) * 256]

        upd = lax.dot_general(bg_b, (xg * wf).astype(BF16), (((0,), (0,)), ((), ())),
                              preferred_element_type=F32)
        state_ref[g] = st * ea[L - 1:L, :] + upd

        zg = z_ref[:, g * 256:(g + 1) * 256]
        y = y * (zg * _sigmoid(zg))
        y = y * _rms_scale(y) * normw_ref[:, g * 256:(g + 1) * 256]
        o_ref[:, g * 256:(g + 1) * 256] = y.astype(o_ref.dtype)

    xin_ref[0:SUBLANES, :] = xin_ref[L:L + SUBLANES, :]
    bcin_ref[0:SUBLANES, :] = bcin_ref[L:L + SUBLANES, :]


def _ssd(proj, conv_w, conv_b, dt_bias, a_log, d_skip, ssd_norm, bsz, seqlen):
    L = SSD_CHUNK
    nc = seqlen // L
    pad = LANES - SSD_HEADS
    dtb = jnp.pad(dt_bias, (0, pad)).reshape(1, LANES)
    alog = jnp.pad(a_log, (0, pad)).reshape(1, LANES)
    dskip = jnp.repeat(d_skip, SSD_HEAD_DIM).reshape(1, D_INNER)
    rows = lambda b, c: b * nc + c
    wblk = D_INNER
    return pl.pallas_call(
        _ssd_kernel,
        grid=(bsz, nc),
        in_specs=[
            pl.BlockSpec((L, wblk), lambda b, c: (rows(b, c), COL_Z // wblk)),
            pl.BlockSpec((L, wblk), lambda b, c: (rows(b, c), COL_X // wblk)),
            pl.BlockSpec((L, wblk), lambda b, c: (rows(b, c), COL_BC // wblk)),
            pl.BlockSpec((L, LANES), lambda b, c: (rows(b, c), COL_DT // LANES)),
            pl.BlockSpec((CONV_K, wblk), lambda b, c: (0, 0)),
            pl.BlockSpec((CONV_K, wblk), lambda b, c: (0, 1)),
            pl.BlockSpec((1, wblk), lambda b, c: (0, 0)),
            pl.BlockSpec((1, wblk), lambda b, c: (0, 1)),
            pl.BlockSpec((1, LANES), lambda b, c: (0, 0)),
            pl.BlockSpec((1, LANES), lambda b, c: (0, 0)),
            pl.BlockSpec((1, D_INNER), lambda b, c: (0, 0)),
            pl.BlockSpec((1, D_INNER), lambda b, c: (0, 0)),
        ],
        out_specs=pl.BlockSpec((L, D_INNER), lambda b, c: (rows(b, c), 0)),
        out_shape=jax.ShapeDtypeStruct((bsz * seqlen, D_INNER), BF16),
        scratch_shapes=[
            pltpu.VMEM((L + SUBLANES, D_INNER), F32),
            pltpu.VMEM((L + SUBLANES, D_INNER), F32),
            pltpu.VMEM((SSD_GROUPS, D_STATE, HEADS_PER_GROUP * SSD_HEAD_DIM), F32),
        ],
        compiler_params=_compiler_params(("parallel", "arbitrary")),
        name="ssd",
    )(proj, proj, proj, proj, conv_w, conv_w, conv_b.reshape(1, -1), conv_b.reshape(1, -1),
      dtb, alog, dskip, ssd_norm.reshape(1, D_INNER))


def _attn_kernel(q_ref, k_ref, v_ref, u_ref, o_ref):
    tq, tk = ATTN_TQ, ATTN_TK
    qi = pl.program_id(2)
    q2 = q_ref[...]
    first_half = lax.broadcasted_iota(jnp.int32, (tq, LANES), 1) < SB_HEAD_DIM
    zero = jnp.zeros_like(q2)
    q_heads = (jnp.where(first_half, q2, zero), jnp.where(first_half, zero, q2))
    u = u_ref[...]
    strictly_below = (lax.broadcasted_iota(jnp.int32, (tq, tk), 1)
                      < lax.broadcasted_iota(jnp.int32, (tq, tk), 0))

    def head_tile(qh, kt, vt, carry, masked):
        z = lax.dot_general(qh, kt, (((1,), (1,)), ((), ())), preferred_element_type=F32)
        sp = jnp.maximum(z, 0.0) + jnp.log2(1.0 + jnp.exp2(-jnp.abs(z)))
        if masked:
            sp = jnp.where(strictly_below, sp, 0.0)
        hi = sp.astype(BF16)
        lo = (sp - hi.astype(F32)).astype(BF16)
        csum = (jnp.dot(hi, u, preferred_element_type=F32)
                + jnp.dot(lo, u, preferred_element_type=F32))
        a = jnp.exp2(z - csum - carry)
        if masked:
            a = jnp.where(strictly_below, a, 0.0)
        out = jnp.dot(a.astype(BF16), vt, preferred_element_type=F32)
        return out, carry + jnp.sum(sp, axis=1, keepdims=True)

    def key_block(kb, carries, acc, masked):
        start = pl.multiple_of(kb * tk, tk)
        kt = k_ref[pl.ds(start, tk), :]
        vt = v_ref[pl.ds(start, tk), :]
        o0, c0 = head_tile(q_heads[0], kt, vt, carries[0], masked)
        o1, c1 = head_tile(q_heads[1], kt, vt, carries[1], masked)
        return (c0, c1), acc + jnp.where(first_half, o0, o1)

    zeros_c = jnp.zeros((tq, 1), F32)
    carries, acc = key_block(qi, (zeros_c, zeros_c), jnp.zeros((tq, LANES), F32), True)

    def body(i, state):
        carries, acc = state
        return key_block(qi - 1 - i, carries, acc, False)

    _, acc = lax.fori_loop(0, qi, body, (carries, acc))
    o_ref[...] = acc.astype(o_ref.dtype)


def _attention(qkv, bsz, seqlen):
    tq, tk = ATTN_TQ, ATTN_TK
    nq = seqlen // tq
    pairs = SB_WIDTH // LANES
    u = (jnp.arange(tk)[:, None] >= jnp.arange(tk)[None, :]).astype(BF16)
    return pl.pallas_call(
        _attn_kernel,
        grid=(bsz, pairs, nq),
        in_specs=[
            pl.BlockSpec((tq, LANES), lambda b, p, i: (b * nq + i, p)),
            pl.BlockSpec((seqlen, LANES), lambda b, p, i: (b, pairs + p)),
            pl.BlockSpec((seqlen, LANES), lambda b, p, i: (b, 2 * pairs + p)),
            pl.BlockSpec((tk, tk), lambda b, p, i: (0, 0)),
        ],
        out_specs=pl.BlockSpec((tq, LANES), lambda b, p, i: (b * nq + i, p)),
        out_shape=jax.ShapeDtypeStruct((bsz * seqlen, SB_WIDTH), BF16),
        compiler_params=_compiler_params(("parallel", "parallel", "arbitrary")),
        name="sb_attention",
    )(qkv, qkv, qkv, u)


def _merge_kernel(x_ref, yssd_ref, ysb_ref, g1_ref, g2_ref, bg_ref, wssd_ref, wsb_ref, wout_ref,
                  nw_ref, o_ref):
    g_ssd = _sigmoid(g1_ref[...] + bg_ref[:, 0:D_MODEL])
    g_sb = _sigmoid(g2_ref[...] + bg_ref[:, D_MODEL:2 * D_MODEL])
    merged = (g_ssd * jnp.dot(yssd_ref[...], wssd_ref[...], preferred_element_type=F32)
              + g_sb * jnp.dot(ysb_ref[...], wsb_ref[...], preferred_element_type=F32))
    o = jnp.dot(merged.astype(BF16), wout_ref[...], preferred_element_type=F32)
    o_ref[...] = x_ref[...] + o * _rms_scale(o) * nw_ref[...]


def _merge(x, y_ssd, y_sb, proj, b_gate, w_ssd, w_sb, w_out, norm_w):
    t, d = x.shape
    tm = MERGE_TM
    const = lambda i: (0, 0)
    return pl.pallas_call(
        _merge_kernel,
        grid=(t // tm,),
        in_specs=[
            pl.BlockSpec((tm, d), lambda i: (i, 0)),
            pl.BlockSpec((tm, D_INNER), lambda i: (i, 0)),
            pl.BlockSpec((tm, SB_WIDTH), lambda i: (i, 0)),
            pl.BlockSpec((tm, d), lambda i: (i, COL_GATE // d)),
            pl.BlockSpec((tm, d), lambda i: (i, COL_GATE // d + 1)),
            pl.BlockSpec((1, 2 * d), const),
            pl.BlockSpec((D_INNER, d), const),
            pl.BlockSpec((SB_WIDTH, d), const),
            pl.BlockSpec((d, d), const),
            pl.BlockSpec((1, d), const),
        ],
        out_specs=pl.BlockSpec((tm, d), lambda i: (i, 0)),
        out_shape=jax.ShapeDtypeStruct((t, d), F32),
        compiler_params=_compiler_params(("parallel",)),
        name="merge",
    )(x, y_ssd, y_sb, proj, proj, b_gate.reshape(1, -1), w_ssd, w_sb, w_out, norm_w.reshape(1, -1))


def _ffn_kernel(h_ref, p_ref, npre_ref, w1_ref, w2_ref, npost_ref, wple_ref, wpg_ref, nple_ref,
                o_ref, n_ref, acc_ref):
    f = pl.program_id(1)

    @pl.when(f == 0)
    def _():
        h = h_ref[...]
        n_ref[...] = (h * _rms_scale(h) * npre_ref[...]).astype(BF16)

    a = jnp.maximum(jnp.dot(n_ref[...], w1_ref[...], preferred_element_type=F32), 0.0)
    part = jnp.dot((a * a).astype(BF16), w2_ref[...], preferred_element_type=F32)

    @pl.when(f == 0)
    def _():
        acc_ref[...] = part

    @pl.when(f > 0)
    def _():
        acc_ref[...] += part

    @pl.when(f == pl.num_programs(1) - 1)
    def _():
        ff = acc_ref[...]
        h2 = h_ref[...] + ff * _rms_scale(ff) * npost_ref[...]
        gate = _sigmoid(jnp.dot(h2.astype(BF16), wpg_ref[...], preferred_element_type=F32))
        pe = gate * jnp.dot(p_ref[...].astype(BF16), wple_ref[...], preferred_element_type=F32)
        o_ref[...] = h2 + pe * _rms_scale(pe) * nple_ref[...]


def _ffn(h, p, n_pre, w1, w2, n_post, w_ple, w_pg, n_ple):
    t, d = h.shape
    tm, tf = FFN_TM, FFN_TF
    const = lambda i, f: (0, 0)
    return pl.pallas_call(
        _ffn_kernel,
        grid=(t // tm, D_FF // tf),
        in_specs=[
            pl.BlockSpec((tm, d), lambda i, f: (i, 0)),
            pl.BlockSpec((tm, PLE_DIM), lambda i, f: (i, 0)),
            pl.BlockSpec((1, d), const),
            pl.BlockSpec((d, tf), lambda i, f: (0, f)),
            pl.BlockSpec((tf, d), lambda i, f: (f, 0)),
            pl.BlockSpec((1, d), const),
            pl.BlockSpec((PLE_DIM, d), const),
            pl.BlockSpec((d, d), const),
            pl.BlockSpec((1, d), const),
        ],
        out_specs=pl.BlockSpec((tm, d), lambda i, f: (i, 0)),
        out_shape=jax.ShapeDtypeStruct((t, d), F32),
        scratch_shapes=[pltpu.VMEM((tm, d), BF16), pltpu.VMEM((tm, d), F32)],
        compiler_params=_compiler_params(("parallel", "arbitrary")),
        name="ffn_ple",
    )(h, p, n_pre.reshape(1, -1), w1, w2, n_post.reshape(1, -1), w_ple, w_pg, n_ple.reshape(1, -1))


def _layer(h, p, norm_mix_pre, w_in, conv_w, conv_b, dt_bias, a_log, d_skip, ssd_norm,
           w_ssd_branch, w_sb_branch, w_gate, b_gate, w_out, norm_mix_post,
           norm_ffn_pre, w_ff1, w_ff2, norm_ffn_post, w_ple, w_ple_gate, norm_ple_post):
    bsz, seqlen, d = h.shape
    t = bsz * seqlen
    x2 = h.reshape(t, d)

    c_dt = D_INNER + D_INNER + 2 * SSD_GROUPS * D_STATE
    c_q = c_dt + SSD_HEADS
    w_dt = jnp.pad(w_in[:, c_dt:c_q], ((0, 0), (0, DT_PAD - SSD_HEADS)))
    w_f32 = jnp.concatenate([w_in[:, :c_dt], w_gate, w_dt], axis=1).astype(BF16)
    q_scale = LOG2E * SB_HEAD_DIM ** -0.5
    w_qkv = jnp.concatenate([w_in[:, c_q:c_q + SB_WIDTH] * q_scale, w_in[:, c_q + SB_WIDTH:]],
                            axis=1).astype(BF16)
    g_pre = norm_mix_pre.reshape(1, d)

    proj = _norm_matmul(x2, g_pre, w_f32, F32, PROJ_TM, PROJ_TN)
    qkv = _norm_matmul(x2, g_pre, w_qkv, BF16, PROJ_TM, PROJ_TN)

    y_ssd = _ssd(proj, conv_w, conv_b, dt_bias, a_log, d_skip, ssd_norm, bsz, seqlen)
    y_sb = _attention(qkv, bsz, seqlen)

    h1 = _merge(x2, y_ssd, y_sb, proj, b_gate, w_ssd_branch.astype(BF16), w_sb_branch.astype(BF16),
                w_out.astype(BF16), norm_mix_post)
    h3 = _ffn(h1, p.reshape(t, PLE_DIM), norm_ffn_pre, w_ff1.astype(BF16), w_ff2.astype(BF16),
              norm_ffn_post, w_ple.astype(BF16), w_ple_gate.astype(BF16), norm_ple_post)
    return h3.reshape(bsz, seqlen, d)


def kernel(x, p, norm_mix_pre, w_in, conv_w, conv_b, dt_bias, a_log, d_skip, ssd_norm, w_ssd_branch, w_sb_branch, w_gate, b_gate, w_out, norm_mix_post, norm_ffn_pre, w_ff1, w_ff2, norm_ffn_post, w_ple, w_ple_gate, norm_ple_post):
    h = x
    for i in range(p.shape[0]):
        h = _layer(h, p[i], norm_mix_pre[i], w_in[i], conv_w[i], conv_b[i], dt_bias[i], a_log[i],
                   d_skip[i], ssd_norm[i], w_ssd_branch[i], w_sb_branch[i], w_gate[i], b_gate[i],
                   w_out[i], norm_mix_post[i], norm_ffn_pre[i], w_ff1[i], w_ff2[i],
                   norm_ffn_post[i], w_ple[i], w_ple_gate[i], norm_ple_post[i])
    return h
```

```python
import functools
import math

import jax
import jax.numpy as jnp
from jax import lax
from jax.experimental import pallas as pl
from jax.experimental.pallas import tpu as pltpu

F32 = jnp.float32
BF16 = jnp.bfloat16

RMS_EPS = 1e-6
LOG2E = math.log2(math.e)

D_MODEL = 1024
D_INNER = 2048
SSD_HEAD_DIM = 64
SSD_HEADS = 32
SSD_GROUPS = 8
HEADS_PER_GROUP = SSD_HEADS // SSD_GROUPS
D_STATE = 128
CONV_K = 4
SSD_CHUNK = 128
SB_HEADS = 16
SB_HEAD_DIM = 64
SB_WIDTH = 1024
D_FF = 4096
PLE_DIM = 256

LANES = 128
SUBLANES = 8
VMEM_LIMIT_BYTES = 56 * 1024 * 1024

PROJ_TM = 1024
PROJ_TN = 768
ATTN_TQ = 256
ATTN_TK = 256
ATTN_PAIRS = 2
MASKED_LOGIT = -1e30
MERGE_TM = 512
FFN_TM = 1024
FFN_TF = 1024

DT_PAD = 256
PROJ_F32_COLS = D_INNER + (D_INNER + 2 * SSD_GROUPS * D_STATE) + 2 * D_MODEL + DT_PAD
COL_Z = 0
COL_X = D_INNER
COL_BC = 2 * D_INNER
COL_GATE = 3 * D_INNER
COL_DT = 3 * D_INNER + 2 * D_MODEL


def _compiler_params(semantics):
    return pltpu.CompilerParams(dimension_semantics=semantics, vmem_limit_bytes=VMEM_LIMIT_BYTES)


def _rms_scale(x):
    return lax.rsqrt(jnp.mean(x * x, axis=-1, keepdims=True) + RMS_EPS)


def _sigmoid(x):
    return 1.0 / (1.0 + jnp.exp(-x))


def _neg_abs(x):
    bits = lax.bitcast_convert_type(x, jnp.int32) | jnp.int32(-2 ** 31)
    return lax.bitcast_convert_type(bits, F32)


def _softplus(x):
    return jnp.maximum(x, 0.0) + jnp.log(1.0 + jnp.exp(-jnp.abs(x)))


def _norm_matmul_kernel(x_ref, g_ref, w_ref, o_ref, n_ref):
    @pl.when(pl.program_id(1) == 0)
    def _():
        x = x_ref[...]
        n_ref[...] = (x * _rms_scale(x) * g_ref[...]).astype(BF16)

    o_ref[...] = jnp.dot(n_ref[...], w_ref[...], preferred_element_type=F32).astype(o_ref.dtype)


def _norm_matmul(x, g, w, out_dtype, tm, tn):
    t, d = x.shape
    n = w.shape[1]
    return pl.pallas_call(
        _norm_matmul_kernel,
        grid=(t // tm, n // tn),
        in_specs=[
            pl.BlockSpec((tm, d), lambda i, j: (i, 0)),
            pl.BlockSpec((1, d), lambda i, j: (0, 0)),
            pl.BlockSpec((d, tn), lambda i, j: (0, j)),
        ],
        out_specs=pl.BlockSpec((tm, tn), lambda i, j: (i, j)),
        out_shape=jax.ShapeDtypeStruct((t, n), out_dtype),
        scratch_shapes=[pltpu.VMEM((tm, d), BF16)],
        compiler_params=_compiler_params(("parallel", "arbitrary")),
        name="norm_matmul",
    )(x, g, w)


def _split3(x):
    p1 = x.astype(BF16)
    r1 = x - p1.astype(F32)
    p2 = r1.astype(BF16)
    r2 = r1 - p2.astype(F32)
    return p1, p2, r2.astype(BF16)


def _ssd_kernel(z_ref, xs_ref, bc_ref, dt_ref, cwx_ref, cwbc_ref, cbx_ref, cbbc_ref,
                dtb_ref, alog_ref, dskip_ref, normw_ref, o_ref,
                xin_ref, bcin_ref, state_ref):
    L = SSD_CHUNK
    c = pl.program_id(1)

    @pl.when(c == 0)
    def _():
        xin_ref[0:SUBLANES, :] = jnp.zeros((SUBLANES, D_INNER), F32)
        bcin_ref[0:SUBLANES, :] = jnp.zeros((SUBLANES, D_INNER), F32)
        state_ref[...] = jnp.zeros_like(state_ref)

    xin_ref[SUBLANES:SUBLANES + L, :] = xs_ref[...]
    bcin_ref[SUBLANES:SUBLANES + L, :] = bc_ref[...]

    dt = _softplus(dt_ref[...] + dtb_ref[...])
    a = -jnp.exp(alog_ref[...])
    row = lax.broadcasted_iota(jnp.int32, (L, L), 0)
    col = lax.broadcasted_iota(jnp.int32, (L, L), 1)
    causal = row >= col
    tril = jnp.where(causal, 1.0, 0.0).astype(BF16)
    p1, p2, p3 = _split3(dt * a)
    a_cs = (jnp.dot(tril, p1, preferred_element_type=F32)
            + jnp.dot(tril, p2, preferred_element_type=F32)
            + jnp.dot(tril, p3, preferred_element_type=F32))
    a_tot = a_cs[L - 1:L, :]
    w_end = dt * jnp.exp(a_tot - a_cs)
    a_cs_t = a_cs.T
    dt_t = dt.T
    w_end_t = w_end.T
    first_half = lax.broadcasted_iota(jnp.int32, (L, LANES), 1) < SSD_HEAD_DIM

    def conv_silu(in_ref, w_ref, b_ref, lo, width):
        acc = b_ref[:, lo:lo + width]
        for k in range(CONV_K):
            off = SUBLANES - (CONV_K - 1) + k
            acc = acc + w_ref[k:k + 1, lo:lo + width] * in_ref[off:off + L, lo:lo + width]
        return acc * _sigmoid(acc)

    def col_bcast(mat_t, h):
        return jnp.broadcast_to(mat_t[h:h + 1, :], (L, L)).T

    for g in range(SSD_GROUPS):
        xg = conv_silu(xin_ref, cwx_ref, cbx_ref, g * 256, 256)
        bg = conv_silu(bcin_ref, cwbc_ref, cbbc_ref, g * D_STATE, D_STATE)
        cg = conv_silu(bcin_ref, cwbc_ref, cbbc_ref, SSD_GROUPS * D_STATE + g * D_STATE, D_STATE)
        xg_b = xg.astype(BF16)
        bg_b = bg.astype(BF16)
        cg_b = cg.astype(BF16)
        cb = lax.dot_general(cg_b, bg_b, (((1,), (1,)), ((), ())), preferred_element_type=F32)
        st = state_ref[g]
        y_off = jnp.dot(cg_b, st.astype(BF16), preferred_element_type=F32)

        y_diag = []
        ea_cols = []
        w_cols = []
        for j in range(HEADS_PER_GROUP):
            h = g * HEADS_PER_GROUP + j
            a_col = col_bcast(a_cs_t, h)
            seg = a_col - a_cs_t[h:h + 1, :]
            decay = jnp.exp(jnp.where(causal, seg, -jnp.inf))
            m = (cb * decay * dt_t[h:h + 1, :]).astype(BF16)
            pair = (j // 2) * LANES
            y_diag.append(jnp.dot(m, xg_b[:, pair:pair + LANES], preferred_element_type=F32))
            ea_cols.append(jnp.exp(a_col))
            w_cols.append(col_bcast(w_end_t, h))
        yd = jnp.concatenate([jnp.where(first_half, y_diag[0], y_diag[1]),
                              jnp.where(first_half, y_diag[2], y_diag[3])], axis=1)
        ea = jnp.concatenate([jnp.where(first_half, ea_cols[0], ea_cols[1]),
                              jnp.where(first_half, ea_cols[2], ea_cols[3])], axis=1)
        wf = jnp.concatenate([jnp.where(first_half, w_cols[0], w_cols[1]),
                              jnp.where(first_half, w_cols[2], w_cols[3])], axis=1)
        y = yd + y_off * ea + xg * dskip_ref[:, g * 256:(g + 1) * 256]

        upd = lax.dot_general(bg_b, (xg * wf).astype(BF16), (((0,), (0,)), ((), ())),
                              preferred_element_type=F32)
        state_ref[g] = st * ea[L - 1:L, :] + upd

        zg = z_ref[:, g * 256:(g + 1) * 256]
        y = y * (zg * _sigmoid(zg))
        y = y * _rms_scale(y) * normw_ref[:, g * 256:(g + 1) * 256]
        o_ref[:, g * 256:(g + 1) * 256] = y.astype(o_ref.dtype)

    xin_ref[0:SUBLANES, :] = xin_ref[L:L + SUBLANES, :]
    bcin_ref[0:SUBLANES, :] = bcin_ref[L:L + SUBLANES, :]


def _ssd(proj, conv_w, conv_b, dt_bias, a_log, d_skip, ssd_norm, bsz, seqlen):
    L = SSD_CHUNK
    nc = seqlen // L
    pad = LANES - SSD_HEADS
    dtb = jnp.pad(dt_bias, (0, pad)).reshape(1, LANES)
    alog = jnp.pad(a_log, (0, pad)).reshape(1, LANES)
    dskip = jnp.repeat(d_skip, SSD_HEAD_DIM).reshape(1, D_INNER)
    rows = lambda b, c: b * nc + c
    wblk = D_INNER
    return pl.pallas_call(
        _ssd_kernel,
        grid=(bsz, nc),
        in_specs=[
            pl.BlockSpec((L, wblk), lambda b, c: (rows(b, c), COL_Z // wblk)),
            pl.BlockSpec((L, wblk), lambda b, c: (rows(b, c), COL_X // wblk)),
            pl.BlockSpec((L, wblk), lambda b, c: (rows(b, c), COL_BC // wblk)),
            pl.BlockSpec((L, LANES), lambda b, c: (rows(b, c), COL_DT // LANES)),
            pl.BlockSpec((CONV_K, wblk), lambda b, c: (0, 0)),
            pl.BlockSpec((CONV_K, wblk), lambda b, c: (0, 1)),
            pl.BlockSpec((1, wblk), lambda b, c: (0, 0)),
            pl.BlockSpec((1, wblk), lambda b, c: (0, 1)),
            pl.BlockSpec((1, LANES), lambda b, c: (0, 0)),
            pl.BlockSpec((1, LANES), lambda b, c: (0, 0)),
            pl.BlockSpec((1, D_INNER), lambda b, c: (0, 0)),
            pl.BlockSpec((1, D_INNER), lambda b, c: (0, 0)),
        ],
        out_specs=pl.BlockSpec((L, D_INNER), lambda b, c: (rows(b, c), 0)),
        out_shape=jax.ShapeDtypeStruct((bsz * seqlen, D_INNER), BF16),
        scratch_shapes=[
            pltpu.VMEM((L + SUBLANES, D_INNER), F32),
            pltpu.VMEM((L + SUBLANES, D_INNER), F32),
            pltpu.VMEM((SSD_GROUPS, D_STATE, HEADS_PER_GROUP * SSD_HEAD_DIM), F32),
        ],
        compiler_params=_compiler_params(("parallel", "arbitrary")),
        name="ssd",
    )(proj, proj, proj, proj, conv_w, conv_w, conv_b.reshape(1, -1), conv_b.reshape(1, -1),
      dtb, alog, dskip, ssd_norm.reshape(1, D_INNER))


def _attn_kernel(q_ref, k_ref, v_ref, u_ref, o_ref, z_scr, hi_scr, lo_scr):
    tq, tk = ATTN_TQ, ATTN_TK
    n_heads = 2 * ATTN_PAIRS
    qi = pl.program_id(2)
    first_half = lax.broadcasted_iota(jnp.int32, (tq, LANES), 1) < SB_HEAD_DIM
    q_heads = []
    for p in range(ATTN_PAIRS):
        q2 = q_ref[:, p * LANES:(p + 1) * LANES]
        zero = jnp.zeros_like(q2)
        q_heads += [jnp.where(first_half, q2, zero), jnp.where(first_half, zero, q2)]
    u = u_ref[...]
    strictly_below = (lax.broadcasted_iota(jnp.int32, (tq, tk), 1)
                      < lax.broadcasted_iota(jnp.int32, (tq, tk), 0))

    def logits_head(kb, slot, h, masked):
        start = pl.multiple_of(kb * tk, tk)
        p = h // 2
        kt = k_ref[pl.ds(start, tk), p * LANES:(p + 1) * LANES]
        z = lax.dot_general(q_heads[h], kt, (((1,), (1,)), ((), ())),
                            preferred_element_type=F32)
        sp = jnp.maximum(z, 0.0) + jnp.log2(1.0 + jnp.exp2(_neg_abs(z)))
        if masked:
            sp = jnp.where(strictly_below, sp, 0.0)
            z = jnp.where(strictly_below, z, MASKED_LOGIT)
        hi = sp.astype(BF16)
        z_scr[slot, h] = z
        hi_scr[slot, h] = hi
        lo_scr[slot, h] = (sp - hi.astype(F32)).astype(BF16)
        return jnp.sum(sp, axis=1, keepdims=True)

    def suffix_head(slot, h):
        return (jnp.dot(hi_scr[slot, h], u, preferred_element_type=F32)
                + jnp.dot(lo_scr[slot, h], u, preferred_element_type=F32))

    def value_head(kb, slot, h, csum, carry):
        start = pl.multiple_of(kb * tk, tk)
        p = h // 2
        a = jnp.exp2(z_scr[slot, h] - csum - carry)
        vt = v_ref[pl.ds(start, tk), p * LANES:(p + 1) * LANES]
        return jnp.dot(a.astype(BF16), vt, preferred_element_type=F32)

    def merge_pairs(acc, outs):
        return tuple(acc[p] + jnp.where(first_half, outs[2 * p], outs[2 * p + 1])
                     for p in range(ATTN_PAIRS))

    zero_carries = tuple(jnp.zeros((tq, 1), F32) for _ in range(n_heads))
    zero_acc = tuple(jnp.zeros((tq, LANES), F32) for _ in range(ATTN_PAIRS))

    def first_step(slot):
        return tuple(logits_head(qi, slot, h, True) for h in range(n_heads))

    def step(kb, old_slot, new_slot, state):
        carries, pending, acc = state
        csums, new_pending, outs = [], [], []
        for t in range(n_heads + 1):
            if t < n_heads:
                csums.append(suffix_head(old_slot, t))
                new_pending.append(logits_head(kb, new_slot, t, False))
            if t >= 1:
                outs.append(value_head(kb + 1, old_slot, t - 1, csums[t - 1], carries[t - 1]))
        carries = tuple(c + s for c, s in zip(carries, pending))
        return carries, tuple(new_pending), merge_pairs(acc, outs)

    def last_step(slot, carries, acc):
        csums = [suffix_head(slot, h) for h in range(n_heads)]
        outs = [value_head(0, slot, h, csums[h], carries[h]) for h in range(n_heads)]
        return merge_pairs(acc, outs)

    def odd_start():
        return step(qi - 1, 1, 0, (zero_carries, first_step(1), zero_acc))

    def even_start():
        return zero_carries, first_step(0), zero_acc

    odd = lax.rem(qi, 2)
    state = lax.cond(odd == 1, odd_start, even_start)
    remaining = qi - odd

    def body(j, state):
        kb = remaining - 1 - 2 * j
        return step(kb - 1, 1, 0, step(kb, 0, 1, state))

    carries, _, acc = lax.fori_loop(0, remaining // 2, body, state)
    acc = last_step(0, carries, acc)
    for p in range(ATTN_PAIRS):
        o_ref[:, p * LANES:(p + 1) * LANES] = acc[p].astype(o_ref.dtype)


def _attention(qkv, bsz, seqlen):
    tq, tk = ATTN_TQ, ATTN_TK
    nq = seqlen // tq
    w = ATTN_PAIRS * LANES
    steps = SB_WIDTH // w
    u = (jnp.arange(tk)[:, None] >= jnp.arange(tk)[None, :]).astype(BF16)
    return pl.pallas_call(
        _attn_kernel,
        grid=(bsz, steps, nq),
        in_specs=[
            pl.BlockSpec((tq, w), lambda b, p, i: (b * nq + i, p)),
            pl.BlockSpec((seqlen, w), lambda b, p, i: (b, steps + p)),
            pl.BlockSpec((seqlen, w), lambda b, p, i: (b, 2 * steps + p)),
            pl.BlockSpec((tk, tk), lambda b, p, i: (0, 0)),
        ],
        out_specs=pl.BlockSpec((tq, w), lambda b, p, i: (b * nq + i, p)),
        out_shape=jax.ShapeDtypeStruct((bsz * seqlen, SB_WIDTH), BF16),
        scratch_shapes=[
            pltpu.VMEM((2, 2 * ATTN_PAIRS, tq, tk), F32),
            pltpu.VMEM((2, 2 * ATTN_PAIRS, tq, tk), BF16),
            pltpu.VMEM((2, 2 * ATTN_PAIRS, tq, tk), BF16),
        ],
        compiler_params=_compiler_params(("parallel", "parallel", "arbitrary")),
        name="sb_attention",
    )(qkv, qkv, qkv, u)


def _merge_kernel(x_ref, yssd_ref, ysb_ref, g1_ref, g2_ref, bg_ref, wssd_ref, wsb_ref, wout_ref,
                  nw_ref, o_ref):
    g_ssd = _sigmoid(g1_ref[...] + bg_ref[:, 0:D_MODEL])
    g_sb = _sigmoid(g2_ref[...] + bg_ref[:, D_MODEL:2 * D_MODEL])
    merged = (g_ssd * jnp.dot(yssd_ref[...], wssd_ref[...], preferred_element_type=F32)
              + g_sb * jnp.dot(ysb_ref[...], wsb_ref[...], preferred_element_type=F32))
    o = jnp.dot(merged.astype(BF16), wout_ref[...], preferred_element_type=F32)
    o_ref[...] = x_ref[...] + o * _rms_scale(o) * nw_ref[...]


def _merge(x, y_ssd, y_sb, proj, b_gate, w_ssd, w_sb, w_out, norm_w):
    t, d = x.shape
    tm = MERGE_TM
    const = lambda i: (0, 0)
    return pl.pallas_call(
        _merge_kernel,
        grid=(t // tm,),
        in_specs=[
            pl.BlockSpec((tm, d), lambda i: (i, 0)),
            pl.BlockSpec((tm, D_INNER), lambda i: (i, 0)),
            pl.BlockSpec((tm, SB_WIDTH), lambda i: (i, 0)),
            pl.BlockSpec((tm, d), lambda i: (i, COL_GATE // d)),
            pl.BlockSpec((tm, d), lambda i: (i, COL_GATE // d + 1)),
            pl.BlockSpec((1, 2 * d), const),
            pl.BlockSpec((D_INNER, d), const),
            pl.BlockSpec((SB_WIDTH, d), const),
            pl.BlockSpec((d, d), const),
            pl.BlockSpec((1, d), const),
        ],
        out_specs=pl.BlockSpec((tm, d), lambda i: (i, 0)),
        out_shape=jax.ShapeDtypeStruct((t, d), F32),
        compiler_params=_compiler_params(("parallel",)),
        name="merge",
    )(x, y_ssd, y_sb, proj, proj, b_gate.reshape(1, -1), w_ssd, w_sb, w_out, norm_w.reshape(1, -1))


def _ffn_kernel(h_ref, p_ref, npre_ref, w1_ref, w2_ref, npost_ref, wple_ref, wpg_ref, nple_ref,
                o_ref, n_ref, acc_ref):
    f = pl.program_id(1)

    @pl.when(f == 0)
    def _():
        h = h_ref[...]
        n_ref[...] = (h * _rms_scale(h) * npre_ref[...]).astype(BF16)

    a = jnp.maximum(jnp.dot(n_ref[...], w1_ref[...], preferred_element_type=F32), 0.0)
    part = jnp.dot((a * a).astype(BF16), w2_ref[...], preferred_element_type=F32)

    @pl.when(f == 0)
    def _():
        acc_ref[...] = part

    @pl.when(f > 0)
    def _():
        acc_ref[...] += part

    @pl.when(f == pl.num_programs(1) - 1)
    def _():
        ff = acc_ref[...]
        h2 = h_ref[...] + ff * _rms_scale(ff) * npost_ref[...]
        gate = _sigmoid(jnp.dot(h2.astype(BF16), wpg_ref[...], preferred_element_type=F32))
        pe = gate * jnp.dot(p_ref[...].astype(BF16), wple_ref[...], preferred_element_type=F32)
        o_ref[...] = h2 + pe * _rms_scale(pe) * nple_ref[...]


def _ffn(h, p, n_pre, w1, w2, n_post, w_ple, w_pg, n_ple):
    t, d = h.shape
    tm, tf = FFN_TM, FFN_TF
    const = lambda i, f: (0, 0)
    return pl.pallas_call(
        _ffn_kernel,
        grid=(t // tm, D_FF // tf),
        in_specs=[
            pl.BlockSpec((tm, d), lambda i, f: (i, 0)),
            pl.BlockSpec((tm, PLE_DIM), lambda i, f: (i, 0)),
            pl.BlockSpec((1, d), const),
            pl.BlockSpec((d, tf), lambda i, f: (0, f)),
            pl.BlockSpec((tf, d), lambda i, f: (f, 0)),
            pl.BlockSpec((1, d), const),
            pl.BlockSpec((PLE_DIM, d), const),
            pl.BlockSpec((d, d), const),
            pl.BlockSpec((1, d), const),
        ],
        out_specs=pl.BlockSpec((tm, d), lambda i, f: (i, 0)),
        out_shape=jax.ShapeDtypeStruct((t, d), F32),
        scratch_shapes=[pltpu.VMEM((tm, d), BF16), pltpu.VMEM((tm, d), F32)],
        compiler_params=_compiler_params(("parallel", "arbitrary")),
        name="ffn_ple",
    )(h, p, n_pre.reshape(1, -1), w1, w2, n_post.reshape(1, -1), w_ple, w_pg, n_ple.reshape(1, -1))


def _layer(h, p, norm_mix_pre, w_in, conv_w, conv_b, dt_bias, a_log, d_skip, ssd_norm,
           w_ssd_branch, w_sb_branch, w_gate, b_gate, w_out, norm_mix_post,
           norm_ffn_pre, w_ff1, w_ff2, norm_ffn_post, w_ple, w_ple_gate, norm_ple_post):
    bsz, seqlen, d = h.shape
    t = bsz * seqlen
    x2 = h.reshape(t, d)

    c_dt = D_INNER + D_INNER + 2 * SSD_GROUPS * D_STATE
    c_q = c_dt + SSD_HEADS
    w_dt = jnp.pad(w_in[:, c_dt:c_q], ((0, 0), (0, DT_PAD - SSD_HEADS)))
    w_f32 = jnp.concatenate([w_in[:, :c_dt], w_gate, w_dt], axis=1).astype(BF16)
    q_scale = LOG2E * SB_HEAD_DIM ** -0.5
    w_qkv = jnp.concatenate([w_in[:, c_q:c_q + SB_WIDTH] * q_scale, w_in[:, c_q + SB_WIDTH:]],
                            axis=1).astype(BF16)
    g_pre = norm_mix_pre.reshape(1, d)

    proj = _norm_matmul(x2, g_pre, w_f32, F32, PROJ_TM, PROJ_TN)
    qkv = _norm_matmul(x2, g_pre, w_qkv, BF16, PROJ_TM, PROJ_TN)

    y_ssd = _ssd(proj, conv_w, conv_b, dt_bias, a_log, d_skip, ssd_norm, bsz, seqlen)
    y_sb = _attention(qkv, bsz, seqlen)

    h1 = _merge(x2, y_ssd, y_sb, proj, b_gate, w_ssd_branch.astype(BF16), w_sb_branch.astype(BF16),
                w_out.astype(BF16), norm_mix_post)
    h3 = _ffn(h1, p.reshape(t, PLE_DIM), norm_ffn_pre, w_ff1.astype(BF16), w_ff2.astype(BF16),
              norm_ffn_post, w_ple.astype(BF16), w_ple_gate.astype(BF16), norm_ple_post)
    return h3.reshape(bsz, seqlen, d)


def kernel(x, p, norm_mix_pre, w_in, conv_w, conv_b, dt_bias, a_log, d_skip, ssd_norm, w_ssd_branch, w_sb_branch, w_gate, b_gate, w_out, norm_mix_post, norm_ffn_pre, w_ff1, w_ff2, norm_ffn_post, w_ple, w_ple_gate, norm_ple_post):
    h = x
    for i in range(p.shape[0]):
        h = _layer(h, p[i], norm_mix_pre[i], w_in[i], conv_w[i], conv_b[i], dt_bias[i], a_log[i],
                   d_skip[i], ssd_norm[i], w_ssd_branch[i], w_sb_branch[i], w_gate[i], b_gate[i],
                   w_out[i], norm_mix_post[i], norm_ffn_pre[i], w_ff1[i], w_ff2[i],
                   norm_ffn_post[i], w_ple[i], w_ple_gate[i], norm_ple_post[i])
    return h
```

```python
import functools
import math

import jax
import jax.numpy as jnp
from jax import lax
from jax.experimental import pallas as pl
from jax.experimental.pallas import tpu as pltpu

F32 = jnp.float32
BF16 = jnp.bfloat16

RMS_EPS = 1e-6
LOG2E = math.log2(math.e)

D_MODEL = 1024
D_INNER = 2048
SSD_HEAD_DIM = 64
SSD_HEADS = 32
SSD_GROUPS = 8
HEADS_PER_GROUP = SSD_HEADS // SSD_GROUPS
D_STATE = 128
CONV_K = 4
SSD_CHUNK = 128
SB_HEADS = 16
SB_HEAD_DIM = 64
SB_WIDTH = 1024
D_FF = 4096
PLE_DIM = 256

LANES = 128
SUBLANES = 8
VMEM_LIMIT_BYTES = 56 * 1024 * 1024

PROJ_TM = 1024
PROJ_TN = 768
ATTN_TQ = 256
ATTN_TK = 256
ATTN_PAIRS = 4
MASKED_LOGIT = -1e30
MERGE_TM = 512
FFN_TM = 1024
FFN_TF = 1024

DT_PAD = 256
PROJ_F32_COLS = D_INNER + (D_INNER + 2 * SSD_GROUPS * D_STATE) + 2 * D_MODEL + DT_PAD
COL_Z = 0
COL_X = D_INNER
COL_BC = 2 * D_INNER
COL_GATE = 3 * D_INNER
COL_DT = 3 * D_INNER + 2 * D_MODEL


def _compiler_params(semantics):
    return pltpu.CompilerParams(dimension_semantics=semantics, vmem_limit_bytes=VMEM_LIMIT_BYTES)


def _rms_scale(x):
    return lax.rsqrt(jnp.mean(x * x, axis=-1, keepdims=True) + RMS_EPS)


def _sigmoid(x):
    return 1.0 / (1.0 + jnp.exp(-x))


def _neg_abs(x):
    bits = lax.bitcast_convert_type(x, jnp.int32) | jnp.int32(-2 ** 31)
    return lax.bitcast_convert_type(bits, F32)


def _softplus(x):
    return jnp.maximum(x, 0.0) + jnp.log(1.0 + jnp.exp(-jnp.abs(x)))


def _norm_matmul_kernel(x_ref, g_ref, w_ref, o_ref, n_ref):
    @pl.when(pl.program_id(1) == 0)
    def _():
        x = x_ref[...]
        n_ref[...] = (x * _rms_scale(x) * g_ref[...]).astype(BF16)

    o_ref[...] = jnp.dot(n_ref[...], w_ref[...], preferred_element_type=F32).astype(o_ref.dtype)


def _norm_matmul(x, g, w, out_dtype, tm, tn):
    t, d = x.shape
    n = w.shape[1]
    return pl.pallas_call(
        _norm_matmul_kernel,
        grid=(t // tm, n // tn),
        in_specs=[
            pl.BlockSpec((tm, d), lambda i, j: (i, 0)),
            pl.BlockSpec((1, d), lambda i, j: (0, 0)),
            pl.BlockSpec((d, tn), lambda i, j: (0, j)),
        ],
        out_specs=pl.BlockSpec((tm, tn), lambda i, j: (i, j)),
        out_shape=jax.ShapeDtypeStruct((t, n), out_dtype),
        scratch_shapes=[pltpu.VMEM((tm, d), BF16)],
        compiler_params=_compiler_params(("parallel", "arbitrary")),
        name="norm_matmul",
    )(x, g, w)


def _split3(x):
    p1 = x.astype(BF16)
    r1 = x - p1.astype(F32)
    p2 = r1.astype(BF16)
    r2 = r1 - p2.astype(F32)
    return p1, p2, r2.astype(BF16)


def _ssd_kernel(z_ref, xs_ref, bc_ref, dt_ref, cwx_ref, cwbc_ref, cbx_ref, cbbc_ref,
                dtb_ref, alog_ref, dskip_ref, normw_ref, o_ref,
                xin_ref, bcin_ref, state_ref):
    L = SSD_CHUNK
    c = pl.program_id(1)

    @pl.when(c == 0)
    def _():
        xin_ref[...] = jnp.zeros_like(xin_ref)
        bcin_ref[...] = jnp.zeros_like(bcin_ref)
        state_ref[...] = jnp.zeros_like(state_ref)

    dt = _softplus(dt_ref[...] + dtb_ref[...])
    a = -jnp.exp(alog_ref[...])
    row = lax.broadcasted_iota(jnp.int32, (L, L), 0)
    col = lax.broadcasted_iota(jnp.int32, (L, L), 1)
    causal = row >= col
    tril = jnp.where(causal, 1.0, 0.0).astype(BF16)
    p1, p2, p3 = _split3(dt * a)
    a_cs = (jnp.dot(tril, p1, preferred_element_type=F32)
            + jnp.dot(tril, p2, preferred_element_type=F32)
            + jnp.dot(tril, p3, preferred_element_type=F32))
    a_tot = a_cs[L - 1:L, :]
    w_end = dt * jnp.exp(a_tot - a_cs)
    a_cs_t = a_cs.T
    dt_t = dt.T
    w_end_t = w_end.T
    first_half = lax.broadcasted_iota(jnp.int32, (L, LANES), 1) < SSD_HEAD_DIM

    def conv_silu(cur_ref, tail_ref, w_ref, b_ref, lo, width):
        cols = slice(lo, lo + width)
        ext = jnp.concatenate([tail_ref[:, cols], cur_ref[:, cols]], axis=0)
        acc = w_ref[0:1, cols] * ext
        for k in range(1, CONV_K):
            acc = w_ref[k:k + 1, cols] * ext + pltpu.roll(acc, 1, 0)
        acc = acc[SUBLANES:, :] + b_ref[:, cols]
        return acc * _sigmoid(acc)

    def col_bcast(mat_t, h):
        return jnp.broadcast_to(mat_t[h:h + 1, :], (L, L)).T

    for g in range(SSD_GROUPS):
        xg = conv_silu(xs_ref, xin_ref, cwx_ref, cbx_ref, g * 256, 256)
        bg = conv_silu(bc_ref, bcin_ref, cwbc_ref, cbbc_ref, g * D_STATE, D_STATE)
        cg = conv_silu(bc_ref, bcin_ref, cwbc_ref, cbbc_ref,
                       SSD_GROUPS * D_STATE + g * D_STATE, D_STATE)
        xg_b = xg.astype(BF16)
        bg_b = bg.astype(BF16)
        cg_b = cg.astype(BF16)
        cb = lax.dot_general(cg_b, bg_b, (((1,), (1,)), ((), ())), preferred_element_type=F32)
        st = state_ref[g]
        y_off = jnp.dot(cg_b, st.astype(BF16), preferred_element_type=F32)

        y_diag = []
        ea_cols = []
        w_cols = []
        for j in range(HEADS_PER_GROUP):
            h = g * HEADS_PER_GROUP + j
            a_col = col_bcast(a_cs_t, h)
            seg = a_col - a_cs_t[h:h + 1, :]
            decay = jnp.exp(jnp.where(causal, seg, -jnp.inf))
            m = (cb * decay * dt_t[h:h + 1, :]).astype(BF16)
            pair = (j // 2) * LANES
            y_diag.append(jnp.dot(m, xg_b[:, pair:pair + LANES], preferred_element_type=F32))
            ea_cols.append(jnp.exp(a_col))
            w_cols.append(col_bcast(w_end_t, h))
        yd = jnp.concatenate([jnp.where(first_half, y_diag[0], y_diag[1]),
                              jnp.where(first_half, y_diag[2], y_diag[3])], axis=1)
        ea = jnp.concatenate([jnp.where(first_half, ea_cols[0], ea_cols[1]),
                              jnp.where(first_half, ea_cols[2], ea_cols[3])], axis=1)
        wf = jnp.concatenate([jnp.where(first_half, w_cols[0], w_cols[1]),
                              jnp.where(first_half, w_cols[2], w_cols[3])], axis=1)
        y = yd + y_off * ea + xg * dskip_ref[:, g * 256:(g + 1) * 256]

        upd = lax.dot_general(bg_b, (xg * wf).astype(BF16), (((0,), (0,)), ((), ())),
                              preferred_element_type=F32)
        state_ref[g] = st * ea[L - 1:L, :] + upd

        zg = z_ref[:, g * 256:(g + 1) * 256]
        y = y * (zg * _sigmoid(zg))
        y = y * _rms_scale(y) * normw_ref[:, g * 256:(g + 1) * 256]
        o_ref[:, g * 256:(g + 1) * 256] = y.astype(o_ref.dtype)

    xin_ref[...] = xs_ref[L - SUBLANES:L, :]
    bcin_ref[...] = bc_ref[L - SUBLANES:L, :]


def _ssd(proj, conv_w, conv_b, dt_bias, a_log, d_skip, ssd_norm, bsz, seqlen):
    L = SSD_CHUNK
    nc = seqlen // L
    pad = LANES - SSD_HEADS
    dtb = jnp.pad(dt_bias, (0, pad)).reshape(1, LANES)
    alog = jnp.pad(a_log, (0, pad)).reshape(1, LANES)
    dskip = jnp.repeat(d_skip, SSD_HEAD_DIM).reshape(1, D_INNER)
    rows = lambda b, c: b * nc + c
    wblk = D_INNER
    return pl.pallas_call(
        _ssd_kernel,
        grid=(bsz, nc),
        in_specs=[
            pl.BlockSpec((L, wblk), lambda b, c: (rows(b, c), COL_Z // wblk)),
            pl.BlockSpec((L, wblk), lambda b, c: (rows(b, c), COL_X // wblk)),
            pl.BlockSpec((L, wblk), lambda b, c: (rows(b, c), COL_BC // wblk)),
            pl.BlockSpec((L, LANES), lambda b, c: (rows(b, c), COL_DT // LANES)),
            pl.BlockSpec((CONV_K, wblk), lambda b, c: (0, 0)),
            pl.BlockSpec((CONV_K, wblk), lambda b, c: (0, 1)),
            pl.BlockSpec((1, wblk), lambda b, c: (0, 0)),
            pl.BlockSpec((1, wblk), lambda b, c: (0, 1)),
            pl.BlockSpec((1, LANES), lambda b, c: (0, 0)),
            pl.BlockSpec((1, LANES), lambda b, c: (0, 0)),
            pl.BlockSpec((1, D_INNER), lambda b, c: (0, 0)),
            pl.BlockSpec((1, D_INNER), lambda b, c: (0, 0)),
        ],
        out_specs=pl.BlockSpec((L, D_INNER), lambda b, c: (rows(b, c), 0)),
        out_shape=jax.ShapeDtypeStruct((bsz * seqlen, D_INNER), BF16),
        scratch_shapes=[
            pltpu.VMEM((SUBLANES, D_INNER), F32),
            pltpu.VMEM((SUBLANES, D_INNER), F32),
            pltpu.VMEM((SSD_GROUPS, D_STATE, HEADS_PER_GROUP * SSD_HEAD_DIM), F32),
        ],
        compiler_params=_compiler_params(("parallel", "arbitrary")),
        name="ssd",
    )(proj, proj, proj, proj, conv_w, conv_w, conv_b.reshape(1, -1), conv_b.reshape(1, -1),
      dtb, alog, dskip, ssd_norm.reshape(1, D_INNER))


def _attn_kernel(q_ref, k_ref, v_ref, u_ref, o_ref, z_scr, sp_scr, acc_scr):
    tq, tk = ATTN_TQ, ATTN_TK
    n_heads = 2 * ATTN_PAIRS
    qi = pl.program_id(2)
    first_half = lax.broadcasted_iota(jnp.int32, (tq, LANES), 1) < SB_HEAD_DIM
    q_heads = []
    for p in range(ATTN_PAIRS):
        q2 = q_ref[:, p * LANES:(p + 1) * LANES]
        zero = jnp.zeros_like(q2)
        q_heads += [jnp.where(first_half, q2, zero), jnp.where(first_half, zero, q2)]
    u = u_ref[...]
    strictly_below = (lax.broadcasted_iota(jnp.int32, (tq, tk), 1)
                      < lax.broadcasted_iota(jnp.int32, (tq, tk), 0))

    def logits_head(kb, slot, h, later, masked):
        start = pl.multiple_of(kb * tk, tk)
        p = h // 2
        kt = k_ref[pl.ds(start, tk), p * LANES:(p + 1) * LANES]
        z = lax.dot_general(q_heads[h], kt, (((1,), (1,)), ((), ())),
                            preferred_element_type=F32)
        sp = jnp.maximum(z, 0.0) + jnp.log2(1.0 + jnp.exp2(_neg_abs(z)))
        if masked:
            sp = jnp.where(strictly_below, sp, 0.0)
            z = jnp.where(strictly_below, z, MASKED_LOGIT)
        z_scr[slot, h] = z if later is None else z - later
        sp_scr[slot, h] = sp.astype(BF16)
        sums = jnp.sum(sp, axis=1, keepdims=True)
        return sums if later is None else later + sums

    def suffix_head(slot, h):
        return jnp.dot(sp_scr[slot, h], u, preferred_element_type=F32)

    def value_head(kb, slot, h, csum):
        start = pl.multiple_of(kb * tk, tk)
        p = h // 2
        a = jnp.exp2(z_scr[slot, h] - csum)
        vt = v_ref[pl.ds(start, tk), p * LANES:(p + 1) * LANES]
        return jnp.dot(a.astype(BF16), vt, preferred_element_type=F32)

    def accumulate(p, out_even, out_odd, first):
        pair = jnp.where(first_half, out_even, out_odd)
        acc_scr[p] = pair if first else acc_scr[p] + pair

    def first_step(slot):
        return tuple(logits_head(qi, slot, h, None, True) for h in range(n_heads))

    def step(kb, old_slot, new_slot, later, first=False):
        new_later, csums, outs = [], [], []
        for t in range(n_heads + 1):
            if t < n_heads:
                csums.append(suffix_head(old_slot, t))
                new_later.append(logits_head(kb, new_slot, t, later[t], False))
            if t >= 1:
                outs.append(value_head(kb + 1, old_slot, t - 1, csums[t - 1]))
                if t % 2 == 0:
                    accumulate(t // 2 - 1, outs[t - 2], outs[t - 1], first)
        return tuple(new_later)

    def last_step(slot, first):
        csums = [suffix_head(slot, h) for h in range(n_heads)]
        for p in range(ATTN_PAIRS):
            accumulate(p, value_head(0, slot, 2 * p, csums[2 * p]),
                       value_head(0, slot, 2 * p + 1, csums[2 * p + 1]), first)

    odd = lax.rem(qi, 2)

    @pl.when(qi == 0)
    def _():
        first_step(0)
        last_step(0, True)

    @pl.when(qi > 0)
    def _():
        def odd_start():
            return step(qi - 1, 1, 0, first_step(1), first=True)

        def even_start():
            later = first_step(0)
            later = step(qi - 1, 0, 1, later, first=True)
            return step(qi - 2, 1, 0, later)

        later = lax.cond(odd == 1, odd_start, even_start)
        remaining = qi - 2 + odd

        def body(j, later):
            kb = remaining - 1 - 2 * j
            return step(kb - 1, 1, 0, step(kb, 0, 1, later))

        lax.fori_loop(0, remaining // 2, body, later)
        last_step(0, False)

    for p in range(ATTN_PAIRS):
        o_ref[:, p * LANES:(p + 1) * LANES] = acc_scr[p].astype(o_ref.dtype)


def _attention(qkv, bsz, seqlen):
    tq, tk = ATTN_TQ, ATTN_TK
    nq = seqlen // tq
    w = ATTN_PAIRS * LANES
    steps = SB_WIDTH // w
    u = (jnp.arange(tk)[:, None] >= jnp.arange(tk)[None, :]).astype(BF16)
    return pl.pallas_call(
        _attn_kernel,
        grid=(bsz, steps, nq),
        in_specs=[
            pl.BlockSpec((tq, w), lambda b, p, i: (b * nq + i, p)),
            pl.BlockSpec((seqlen, w), lambda b, p, i: (b, steps + p)),
            pl.BlockSpec((seqlen, w), lambda b, p, i: (b, 2 * steps + p)),
            pl.BlockSpec((tk, tk), lambda b, p, i: (0, 0)),
        ],
        out_specs=pl.BlockSpec((tq, w), lambda b, p, i: (b * nq + i, p)),
        out_shape=jax.ShapeDtypeStruct((bsz * seqlen, SB_WIDTH), BF16),
        scratch_shapes=[
            pltpu.VMEM((2, 2 * ATTN_PAIRS, tq, tk), F32),
            pltpu.VMEM((2, 2 * ATTN_PAIRS, tq, tk), BF16),
            pltpu.VMEM((ATTN_PAIRS, tq, LANES), F32),
        ],
        compiler_params=_compiler_params(("parallel", "parallel", "arbitrary")),
        name="sb_attention",
    )(qkv, qkv, qkv, u)


def _merge_kernel(x_ref, yssd_ref, ysb_ref, g1_ref, g2_ref, bg_ref, wssd_ref, wsb_ref, wout_ref,
                  nw_ref, o_ref):
    g_ssd = _sigmoid(g1_ref[...] + bg_ref[:, 0:D_MODEL])
    g_sb = _sigmoid(g2_ref[...] + bg_ref[:, D_MODEL:2 * D_MODEL])
    merged = (g_ssd * jnp.dot(yssd_ref[...], wssd_ref[...], preferred_element_type=F32)
              + g_sb * jnp.dot(ysb_ref[...], wsb_ref[...], preferred_element_type=F32))
    o = jnp.dot(merged.astype(BF16), wout_ref[...], preferred_element_type=F32)
    o_ref[...] = x_ref[...] + o * _rms_scale(o) * nw_ref[...]


def _merge(x, y_ssd, y_sb, proj, b_gate, w_ssd, w_sb, w_out, norm_w):
    t, d = x.shape
    tm = MERGE_TM
    const = lambda i: (0, 0)
    return pl.pallas_call(
        _merge_kernel,
        grid=(t // tm,),
        in_specs=[
            pl.BlockSpec((tm, d), lambda i: (i, 0)),
            pl.BlockSpec((tm, D_INNER), lambda i: (i, 0)),
            pl.BlockSpec((tm, SB_WIDTH), lambda i: (i, 0)),
            pl.BlockSpec((tm, d), lambda i: (i, COL_GATE // d)),
            pl.BlockSpec((tm, d), lambda i: (i, COL_GATE // d + 1)),
            pl.BlockSpec((1, 2 * d), const),
            pl.BlockSpec((D_INNER, d), const),
            pl.BlockSpec((SB_WIDTH, d), const),
            pl.BlockSpec((d, d), const),
            pl.BlockSpec((1, d), const),
        ],
        out_specs=pl.BlockSpec((tm, d), lambda i: (i, 0)),
        out_shape=jax.ShapeDtypeStruct((t, d), F32),
        compiler_params=_compiler_params(("parallel",)),
        name="merge",
    )(x, y_ssd, y_sb, proj, proj, b_gate.reshape(1, -1), w_ssd, w_sb, w_out, norm_w.reshape(1, -1))


def _ffn_kernel(h_ref, p_ref, npre_ref, w1_ref, w2_ref, npost_ref, wple_ref, wpg_ref, nple_ref,
                o_ref, n_ref, acc_ref):
    f = pl.program_id(1)

    @pl.when(f == 0)
    def _():
        h = h_ref[...]
        n_ref[...] = (h * _rms_scale(h) * npre_ref[...]).astype(BF16)

    a = jnp.maximum(jnp.dot(n_ref[...], w1_ref[...], preferred_element_type=F32), 0.0)
    part = jnp.dot((a * a).astype(BF16), w2_ref[...], preferred_element_type=F32)

    @pl.when(f == 0)
    def _():
        acc_ref[...] = part

    @pl.when(f > 0)
    def _():
        acc_ref[...] += part

    @pl.when(f == pl.num_programs(1) - 1)
    def _():
        ff = acc_ref[...]
        h2 = h_ref[...] + ff * _rms_scale(ff) * npost_ref[...]
        gate = _sigmoid(jnp.dot(h2.astype(BF16), wpg_ref[...], preferred_element_type=F32))
        pe = gate * jnp.dot(p_ref[...].astype(BF16), wple_ref[...], preferred_element_type=F32)
        o_ref[...] = h2 + pe * _rms_scale(pe) * nple_ref[...]


def _ffn(h, p, n_pre, w1, w2, n_post, w_ple, w_pg, n_ple):
    t, d = h.shape
    tm, tf = FFN_TM, FFN_TF
    const = lambda i, f: (0, 0)
    return pl.pallas_call(
        _ffn_kernel,
        grid=(t // tm, D_FF // tf),
        in_specs=[
            pl.BlockSpec((tm, d), lambda i, f: (i, 0)),
            pl.BlockSpec((tm, PLE_DIM), lambda i, f: (i, 0)),
            pl.BlockSpec((1, d), const),
            pl.BlockSpec((d, tf), lambda i, f: (0, f)),
            pl.BlockSpec((tf, d), lambda i, f: (f, 0)),
            pl.BlockSpec((1, d), const),
            pl.BlockSpec((PLE_DIM, d), const),
            pl.BlockSpec((d, d), const),
            pl.BlockSpec((1, d), const),
        ],
        out_specs=pl.BlockSpec((tm, d), lambda i, f: (i, 0)),
        out_shape=jax.ShapeDtypeStruct((t, d), F32),
        scratch_shapes=[pltpu.VMEM((tm, d), BF16), pltpu.VMEM((tm, d), F32)],
        compiler_params=_compiler_params(("parallel", "arbitrary")),
        name="ffn_ple",
    )(h, p, n_pre.reshape(1, -1), w1, w2, n_post.reshape(1, -1), w_ple, w_pg, n_ple.reshape(1, -1))


def _layer(h, p, norm_mix_pre, w_in, conv_w, conv_b, dt_bias, a_log, d_skip, ssd_norm,
           w_ssd_branch, w_sb_branch, w_gate, b_gate, w_out, norm_mix_post,
           norm_ffn_pre, w_ff1, w_ff2, norm_ffn_post, w_ple, w_ple_gate, norm_ple_post):
    bsz, seqlen, d = h.shape
    t = bsz * seqlen
    x2 = h.reshape(t, d)

    c_dt = D_INNER + D_INNER + 2 * SSD_GROUPS * D_STATE
    c_q = c_dt + SSD_HEADS
    w_dt = jnp.pad(w_in[:, c_dt:c_q], ((0, 0), (0, DT_PAD - SSD_HEADS)))
    w_f32 = jnp.concatenate([w_in[:, :c_dt], w_gate, w_dt], axis=1).astype(BF16)
    q_scale = LOG2E * SB_HEAD_DIM ** -0.5
    w_qkv = jnp.concatenate([w_in[:, c_q:c_q + SB_WIDTH] * q_scale, w_in[:, c_q + SB_WIDTH:]],
                            axis=1).astype(BF16)
    g_pre = norm_mix_pre.reshape(1, d)

    proj = _norm_matmul(x2, g_pre, w_f32, F32, PROJ_TM, PROJ_TN)
    qkv = _norm_matmul(x2, g_pre, w_qkv, BF16, PROJ_TM, PROJ_TN)

    y_ssd = _ssd(proj, conv_w, conv_b, dt_bias, a_log, d_skip, ssd_norm, bsz, seqlen)
    y_sb = _attention(qkv, bsz, seqlen)

    h1 = _merge(x2, y_ssd, y_sb, proj, b_gate, w_ssd_branch.astype(BF16), w_sb_branch.astype(BF16),
                w_out.astype(BF16), norm_mix_post)
    h3 = _ffn(h1, p.reshape(t, PLE_DIM), norm_ffn_pre, w_ff1.astype(BF16), w_ff2.astype(BF16),
              norm_ffn_post, w_ple.astype(BF16), w_ple_gate.astype(BF16), norm_ple_post)
    return h3.reshape(bsz, seqlen, d)


def kernel(x, p, norm_mix_pre, w_in, conv_w, conv_b, dt_bias, a_log, d_skip, ssd_norm, w_ssd_branch, w_sb_branch, w_gate, b_gate, w_out, norm_mix_post, norm_ffn_pre, w_ff1, w_ff2, norm_ffn_post, w_ple, w_ple_gate, norm_ple_post):
    h = x
    for i in range(p.shape[0]):
        h = _layer(h, p[i], norm_mix_pre[i], w_in[i], conv_w[i], conv_b[i], dt_bias[i], a_log[i],
                   d_skip[i], ssd_norm[i], w_ssd_branch[i], w_sb_branch[i], w_gate[i], b_gate[i],
                   w_out[i], norm_mix_post[i], norm_ffn_pre[i], w_ff1[i], w_ff2[i],
                   norm_ffn_post[i], w_ple[i], w_ple_gate[i], norm_ple_post[i])
    return h
```

```python
import functools
import math

import jax
import jax.numpy as jnp
from jax import lax
from jax.experimental import pallas as pl
from jax.experimental.pallas import tpu as pltpu

F32 = jnp.float32
BF16 = jnp.bfloat16

RMS_EPS = 1e-6
LOG2E = math.log2(math.e)

D_MODEL = 1024
D_INNER = 2048
SSD_HEAD_DIM = 64
SSD_HEADS = 32
SSD_GROUPS = 8
HEADS_PER_GROUP = SSD_HEADS // SSD_GROUPS
D_STATE = 128
CONV_K = 4
SSD_CHUNK = 128
SB_HEADS = 16
SB_HEAD_DIM = 64
SB_WIDTH = 1024
D_FF = 4096
PLE_DIM = 256

LANES = 128
SUBLANES = 8
VMEM_LIMIT_BYTES = 56 * 1024 * 1024

PROJ_TM = 1024
PROJ_TN = 1280
QKV_TN = 1536
ATTN_TQ = 256
ATTN_TK = 256
ATTN_PAIRS = 4
MASKED_LOGIT = -1e30
MERGE_TM = 512
FFN_TM = 1024
FFN_TF = 1024

DT_PAD = 256
PROJ_F32_COLS = D_INNER + (D_INNER + 2 * SSD_GROUPS * D_STATE) + DT_PAD
COL_Z = 0
COL_X = D_INNER
COL_BC = 2 * D_INNER
COL_DT = 3 * D_INNER


def _compiler_params(semantics):
    return pltpu.CompilerParams(dimension_semantics=semantics, vmem_limit_bytes=VMEM_LIMIT_BYTES)


def _rms_scale(x):
    return lax.rsqrt(jnp.mean(x * x, axis=-1, keepdims=True) + RMS_EPS)


def _sigmoid(x):
    return 1.0 / (1.0 + jnp.exp(-x))


def _neg_abs(x):
    bits = lax.bitcast_convert_type(x, jnp.int32) | jnp.int32(-2 ** 31)
    return lax.bitcast_convert_type(bits, F32)


def _softplus(x):
    return jnp.maximum(x, 0.0) + jnp.log(1.0 + jnp.exp(-jnp.abs(x)))


def _norm_matmul_kernel(x_ref, g_ref, w_ref, o_ref, n_ref):
    @pl.when(pl.program_id(1) == 0)
    def _():
        x = x_ref[...]
        n_ref[...] = (x * _rms_scale(x) * g_ref[...]).astype(BF16)

    o_ref[...] = jnp.dot(n_ref[...], w_ref[...], preferred_element_type=F32).astype(o_ref.dtype)


def _norm_matmul(x, g, w, out_dtype, tm, tn):
    t, d = x.shape
    n = w.shape[1]
    return pl.pallas_call(
        _norm_matmul_kernel,
        grid=(t // tm, n // tn),
        in_specs=[
            pl.BlockSpec((tm, d), lambda i, j: (i, 0)),
            pl.BlockSpec((1, d), lambda i, j: (0, 0)),
            pl.BlockSpec((d, tn), lambda i, j: (0, j)),
        ],
        out_specs=pl.BlockSpec((tm, tn), lambda i, j: (i, j)),
        out_shape=jax.ShapeDtypeStruct((t, n), out_dtype),
        scratch_shapes=[pltpu.VMEM((tm, d), BF16)],
        compiler_params=_compiler_params(("parallel", "arbitrary")),
        name="norm_matmul",
    )(x, g, w)


def _split3(x):
    p1 = x.astype(BF16)
    r1 = x - p1.astype(F32)
    p2 = r1.astype(BF16)
    r2 = r1 - p2.astype(F32)
    return p1, p2, r2.astype(BF16)


def _ssd_kernel(z_ref, xs_ref, bc_ref, dt_ref, cwx_ref, cwbc_ref, cbx_ref, cbbc_ref,
                dtb_ref, alog_ref, dskip_ref, normw_ref, wout_ref, o_ref,
                xin_ref, bcin_ref, state_ref, y_ref):
    c = pl.program_id(1)
    last = pl.num_programs(1) - 1

    def project_previous():
        o_ref[...] = jnp.dot(y_ref[...], wout_ref[...], preferred_element_type=F32)

    @pl.when(c == 0)
    def _():
        y_ref[...] = jnp.zeros_like(y_ref)
        xin_ref[...] = jnp.zeros_like(xin_ref)
        bcin_ref[...] = jnp.zeros_like(bcin_ref)
        state_ref[...] = jnp.zeros_like(state_ref)

    @pl.when(c < last)
    def _():
        _ssd_chunk(project_previous, z_ref, xs_ref, bc_ref, dt_ref, cwx_ref, cwbc_ref, cbx_ref,
                   cbbc_ref, dtb_ref, alog_ref, dskip_ref, normw_ref, xin_ref, bcin_ref, state_ref,
                   y_ref)

    @pl.when(c == last)
    def _():
        project_previous()


def _ssd_chunk(project_previous, z_ref, xs_ref, bc_ref, dt_ref, cwx_ref, cwbc_ref, cbx_ref,
               cbbc_ref, dtb_ref, alog_ref, dskip_ref, normw_ref, xin_ref, bcin_ref, state_ref,
               y_ref):
    L = SSD_CHUNK

    dt = _softplus(dt_ref[...] + dtb_ref[...])
    a = -jnp.exp(alog_ref[...])
    row = lax.broadcasted_iota(jnp.int32, (L, L), 0)
    col = lax.broadcasted_iota(jnp.int32, (L, L), 1)
    causal = row >= col
    tril = jnp.where(causal, 1.0, 0.0).astype(BF16)
    p1, p2, p3 = _split3(dt * a)
    a_cs = (jnp.dot(tril, p1, preferred_element_type=F32)
            + jnp.dot(tril, p2, preferred_element_type=F32)
            + jnp.dot(tril, p3, preferred_element_type=F32))
    a_tot = a_cs[L - 1:L, :]
    w_end = dt * jnp.exp(a_tot - a_cs)
    a_cs_t = a_cs.T
    dt_t = dt.T
    w_end_t = w_end.T
    first_half = lax.broadcasted_iota(jnp.int32, (L, LANES), 1) < SSD_HEAD_DIM
    project_previous()

    def conv_silu(cur_ref, tail_ref, w_ref, b_ref, lo, width):
        cols = slice(lo, lo + width)
        ext = jnp.concatenate([tail_ref[:, cols], cur_ref[:, cols]], axis=0)
        acc = w_ref[0:1, cols] * ext
        for k in range(1, CONV_K):
            acc = w_ref[k:k + 1, cols] * ext + pltpu.roll(acc, 1, 0)
        acc = acc[SUBLANES:, :] + b_ref[:, cols]
        return acc * _sigmoid(acc)

    def col_bcast(mat_t, h):
        return jnp.broadcast_to(mat_t[h:h + 1, :], (L, L)).T

    for g in range(SSD_GROUPS):
        xg = conv_silu(xs_ref, xin_ref, cwx_ref, cbx_ref, g * 256, 256)
        bg = conv_silu(bc_ref, bcin_ref, cwbc_ref, cbbc_ref, g * D_STATE, D_STATE)
        cg = conv_silu(bc_ref, bcin_ref, cwbc_ref, cbbc_ref,
                       SSD_GROUPS * D_STATE + g * D_STATE, D_STATE)
        xg_b = xg.astype(BF16)
        bg_b = bg.astype(BF16)
        cg_b = cg.astype(BF16)
        cb = lax.dot_general(cg_b, bg_b, (((1,), (1,)), ((), ())), preferred_element_type=F32)
        st = state_ref[g]
        y_off = jnp.dot(cg_b, st.astype(BF16), preferred_element_type=F32)

        y_diag = []
        ea_cols = []
        w_cols = []
        for j in range(HEADS_PER_GROUP):
            h = g * HEADS_PER_GROUP + j
            a_col = col_bcast(a_cs_t, h)
            seg = a_col - a_cs_t[h:h + 1, :]
            decay = jnp.exp(jnp.where(causal, seg, -jnp.inf))
            m = (cb * decay * dt_t[h:h + 1, :]).astype(BF16)
            pair = (j // 2) * LANES
            y_diag.append(jnp.dot(m, xg_b[:, pair:pair + LANES], preferred_element_type=F32))
            ea_cols.append(jnp.exp(a_col))
            w_cols.append(col_bcast(w_end_t, h))
        yd = jnp.concatenate([jnp.where(first_half, y_diag[0], y_diag[1]),
                              jnp.where(first_half, y_diag[2], y_diag[3])], axis=1)
        ea = jnp.concatenate([jnp.where(first_half, ea_cols[0], ea_cols[1]),
                              jnp.where(first_half, ea_cols[2], ea_cols[3])], axis=1)
        wf = jnp.concatenate([jnp.where(first_half, w_cols[0], w_cols[1]),
                              jnp.where(first_half, w_cols[2], w_cols[3])], axis=1)
        y = yd + y_off * ea + xg * dskip_ref[:, g * 256:(g + 1) * 256]

        upd = lax.dot_general(bg_b, (xg * wf).astype(BF16), (((0,), (0,)), ((), ())),
                              preferred_element_type=F32)
        state_ref[g] = st * ea[L - 1:L, :] + upd

        zg = z_ref[:, g * 256:(g + 1) * 256]
        y = y * (zg * _sigmoid(zg))
        y = y * _rms_scale(y) * normw_ref[:, g * 256:(g + 1) * 256]
        y_ref[:, g * 256:(g + 1) * 256] = y.astype(y_ref.dtype)

    xin_ref[...] = xs_ref[L - SUBLANES:L, :]
    bcin_ref[...] = bc_ref[L - SUBLANES:L, :]


def _ssd(proj, conv_w, conv_b, dt_bias, a_log, d_skip, ssd_norm, w_branch, bsz, seqlen):
    L = SSD_CHUNK
    nc = seqlen // L
    steps = nc + 1
    pad = LANES - SSD_HEADS
    dtb = jnp.pad(dt_bias, (0, pad)).reshape(1, LANES)
    alog = jnp.pad(a_log, (0, pad)).reshape(1, LANES)
    dskip = jnp.repeat(d_skip, SSD_HEAD_DIM).reshape(1, D_INNER)
    rows = lambda b, c: b * nc + jnp.minimum(c, nc - 1)
    out_rows = lambda b, c: b * nc + jnp.maximum(c - 1, 0)
    wblk = D_INNER
    return pl.pallas_call(
        _ssd_kernel,
        grid=(bsz, steps),
        in_specs=[
            pl.BlockSpec((L, wblk), lambda b, c: (rows(b, c), COL_Z // wblk)),
            pl.BlockSpec((L, wblk), lambda b, c: (rows(b, c), COL_X // wblk)),
            pl.BlockSpec((L, wblk), lambda b, c: (rows(b, c), COL_BC // wblk)),
            pl.BlockSpec((L, LANES), lambda b, c: (rows(b, c), COL_DT // LANES)),
            pl.BlockSpec((CONV_K, wblk), lambda b, c: (0, 0)),
            pl.BlockSpec((CONV_K, wblk), lambda b, c: (0, 1)),
            pl.BlockSpec((1, wblk), lambda b, c: (0, 0)),
            pl.BlockSpec((1, wblk), lambda b, c: (0, 1)),
            pl.BlockSpec((1, LANES), lambda b, c: (0, 0)),
            pl.BlockSpec((1, LANES), lambda b, c: (0, 0)),
            pl.BlockSpec((1, D_INNER), lambda b, c: (0, 0)),
            pl.BlockSpec((1, D_INNER), lambda b, c: (0, 0)),
            pl.BlockSpec((D_INNER, D_MODEL), lambda b, c: (0, 0)),
        ],
        out_specs=pl.BlockSpec((L, D_MODEL), lambda b, c: (out_rows(b, c), 0)),
        out_shape=jax.ShapeDtypeStruct((bsz * seqlen, D_MODEL), F32),
        scratch_shapes=[
            pltpu.VMEM((SUBLANES, D_INNER), F32),
            pltpu.VMEM((SUBLANES, D_INNER), F32),
            pltpu.VMEM((SSD_GROUPS, D_STATE, HEADS_PER_GROUP * SSD_HEAD_DIM), F32),
            pltpu.VMEM((L, D_INNER), BF16),
        ],
        compiler_params=_compiler_params(("parallel", "arbitrary")),
        name="ssd",
    )(proj, proj, proj, proj, conv_w, conv_w, conv_b.reshape(1, -1), conv_b.reshape(1, -1),
      dtb, alog, dskip, ssd_norm.reshape(1, D_INNER), w_branch)


def _attn_kernel(q_ref, k_ref, v_ref, u_ref, o_ref, z_scr, sp_scr, acc_scr):
    tq, tk = ATTN_TQ, ATTN_TK
    n_heads = 2 * ATTN_PAIRS
    qi = pl.program_id(2)
    first_half = lax.broadcasted_iota(jnp.int32, (tq, LANES), 1) < SB_HEAD_DIM
    q_heads = []
    for p in range(ATTN_PAIRS):
        q2 = q_ref[:, p * LANES:(p + 1) * LANES]
        zero = jnp.zeros_like(q2)
        q_heads += [jnp.where(first_half, q2, zero), jnp.where(first_half, zero, q2)]
    u = u_ref[...]
    strictly_below = (lax.broadcasted_iota(jnp.int32, (tq, tk), 1)
                      < lax.broadcasted_iota(jnp.int32, (tq, tk), 0))

    def logits_head(kb, slot, h, later, masked):
        start = pl.multiple_of(kb * tk, tk)
        p = h // 2
        kt = k_ref[pl.ds(start, tk), p * LANES:(p + 1) * LANES]
        z = lax.dot_general(q_heads[h], kt, (((1,), (1,)), ((), ())),
                            preferred_element_type=F32)
        sp = jnp.maximum(z, 0.0) + jnp.log2(1.0 + jnp.exp2(_neg_abs(z)))
        if masked:
            sp = jnp.where(strictly_below, sp, 0.0)
            z = jnp.where(strictly_below, z, MASKED_LOGIT)
        z_scr[slot, h] = z if later is None else z - later
        sp_scr[slot, h] = sp.astype(BF16)
        sums = jnp.sum(sp, axis=1, keepdims=True)
        return sums if later is None else later + sums

    def suffix_head(slot, h):
        return jnp.dot(sp_scr[slot, h], u, preferred_element_type=F32)

    def value_head(kb, slot, h, csum):
        start = pl.multiple_of(kb * tk, tk)
        p = h // 2
        a = jnp.exp2(z_scr[slot, h] - csum)
        vt = v_ref[pl.ds(start, tk), p * LANES:(p + 1) * LANES]
        return jnp.dot(a.astype(BF16), vt, preferred_element_type=F32)

    def accumulate(p, out_even, out_odd, first):
        pair = jnp.where(first_half, out_even, out_odd)
        acc_scr[p] = pair if first else acc_scr[p] + pair

    def first_step(slot):
        return tuple(logits_head(qi, slot, h, None, True) for h in range(n_heads))

    def step(kb, old_slot, new_slot, later, first=False):
        new_later, csums, outs = [], [], []
        for t in range(n_heads + 1):
            if t < n_heads:
                csums.append(suffix_head(old_slot, t))
                new_later.append(logits_head(kb, new_slot, t, later[t], False))
            if t >= 1:
                outs.append(value_head(kb + 1, old_slot, t - 1, csums[t - 1]))
                if t % 2 == 0:
                    accumulate(t // 2 - 1, outs[t - 2], outs[t - 1], first)
        return tuple(new_later)

    def last_step(slot, first):
        csums = [suffix_head(slot, h) for h in range(n_heads)]
        for p in range(ATTN_PAIRS):
            accumulate(p, value_head(0, slot, 2 * p, csums[2 * p]),
                       value_head(0, slot, 2 * p + 1, csums[2 * p + 1]), first)

    odd = lax.rem(qi, 2)

    @pl.when(qi == 0)
    def _():
        first_step(0)
        last_step(0, True)

    @pl.when(qi > 0)
    def _():
        def odd_start():
            return step(qi - 1, 1, 0, first_step(1), first=True)

        def even_start():
            later = first_step(0)
            later = step(qi - 1, 0, 1, later, first=True)
            return step(qi - 2, 1, 0, later)

        later = lax.cond(odd == 1, odd_start, even_start)
        remaining = qi - 2 + odd

        def body(j, later):
            kb = remaining - 1 - 2 * j
            return step(kb - 1, 1, 0, step(kb, 0, 1, later))

        lax.fori_loop(0, remaining // 2, body, later)
        last_step(0, False)

    for p in range(ATTN_PAIRS):
        o_ref[:, p * LANES:(p + 1) * LANES] = acc_scr[p].astype(o_ref.dtype)


def _attention(qkv, bsz, seqlen):
    tq, tk = ATTN_TQ, ATTN_TK
    nq = seqlen // tq
    w = ATTN_PAIRS * LANES
    steps = SB_WIDTH // w
    u = (jnp.arange(tk)[:, None] >= jnp.arange(tk)[None, :]).astype(BF16)
    return pl.pallas_call(
        _attn_kernel,
        grid=(bsz, steps, nq),
        in_specs=[
            pl.BlockSpec((tq, w), lambda b, p, i: (b * nq + i, p)),
            pl.BlockSpec((seqlen, w), lambda b, p, i: (b, steps + p)),
            pl.BlockSpec((seqlen, w), lambda b, p, i: (b, 2 * steps + p)),
            pl.BlockSpec((tk, tk), lambda b, p, i: (0, 0)),
        ],
        out_specs=pl.BlockSpec((tq, w), lambda b, p, i: (b * nq + i, p)),
        out_shape=jax.ShapeDtypeStruct((bsz * seqlen, SB_WIDTH), BF16),
        scratch_shapes=[
            pltpu.VMEM((2, 2 * ATTN_PAIRS, tq, tk), F32),
            pltpu.VMEM((2, 2 * ATTN_PAIRS, tq, tk), BF16),
            pltpu.VMEM((ATTN_PAIRS, tq, LANES), F32),
        ],
        compiler_params=_compiler_params(("parallel", "parallel", "arbitrary")),
        name="sb_attention",
    )(qkv, qkv, qkv, u)


def _merge_kernel(x_ref, pssd_ref, ysb_ref, npre_ref, wg_ref, bg_ref, wsb_ref, wout_ref,
                  nw_ref, o_ref):
    x = x_ref[...]
    n1 = (x * _rms_scale(x) * npre_ref[...]).astype(BF16)
    gates = _sigmoid(jnp.dot(n1, wg_ref[...], preferred_element_type=F32) + bg_ref[...])
    merged = (gates[:, 0:D_MODEL] * pssd_ref[...]
              + gates[:, D_MODEL:2 * D_MODEL]
              * jnp.dot(ysb_ref[...], wsb_ref[...], preferred_element_type=F32))
    o = jnp.dot(merged.astype(BF16), wout_ref[...], preferred_element_type=F32)
    o_ref[...] = x + o * _rms_scale(o) * nw_ref[...]


def _merge(x, p_ssd, y_sb, norm_pre, w_gate, b_gate, w_sb, w_out, norm_w):
    t, d = x.shape
    tm = MERGE_TM
    const = lambda i: (0, 0)
    return pl.pallas_call(
        _merge_kernel,
        grid=(t // tm,),
        in_specs=[
            pl.BlockSpec((tm, d), lambda i: (i, 0)),
            pl.BlockSpec((tm, d), lambda i: (i, 0)),
            pl.BlockSpec((tm, SB_WIDTH), lambda i: (i, 0)),
            pl.BlockSpec((1, d), const),
            pl.BlockSpec((d, 2 * d), const),
            pl.BlockSpec((1, 2 * d), const),
            pl.BlockSpec((SB_WIDTH, d), const),
            pl.BlockSpec((d, d), const),
            pl.BlockSpec((1, d), const),
        ],
        out_specs=pl.BlockSpec((tm, d), lambda i: (i, 0)),
        out_shape=jax.ShapeDtypeStruct((t, d), F32),
        compiler_params=_compiler_params(("parallel",)),
        name="merge",
    )(x, p_ssd, y_sb, norm_pre, w_gate, b_gate.reshape(1, -1), w_sb, w_out, norm_w.reshape(1, -1))


def _ffn_kernel(h_ref, p_ref, npre_ref, w1_ref, w2_ref, npost_ref, wple_ref, wpg_ref, nple_ref,
                o_ref, n_ref, acc_ref):
    f = pl.program_id(1)

    @pl.when(f == 0)
    def _():
        h = h_ref[...]
        n_ref[...] = (h * _rms_scale(h) * npre_ref[...]).astype(BF16)

    a = jnp.maximum(jnp.dot(n_ref[...], w1_ref[...], preferred_element_type=F32), 0.0)
    part = jnp.dot((a * a).astype(BF16), w2_ref[...], preferred_element_type=F32)

    @pl.when(f == 0)
    def _():
        acc_ref[...] = part

    @pl.when(f > 0)
    def _():
        acc_ref[...] += part

    @pl.when(f == pl.num_programs(1) - 1)
    def _():
        ff = acc_ref[...]
        h2 = h_ref[...] + ff * _rms_scale(ff) * npost_ref[...]
        gate = _sigmoid(jnp.dot(h2.astype(BF16), wpg_ref[...], preferred_element_type=F32))
        pe = gate * jnp.dot(p_ref[...].astype(BF16), wple_ref[...], preferred_element_type=F32)
        o_ref[...] = h2 + pe * _rms_scale(pe) * nple_ref[...]


def _ffn(h, p, n_pre, w1, w2, n_post, w_ple, w_pg, n_ple):
    t, d = h.shape
    tm, tf = FFN_TM, FFN_TF
    const = lambda i, f: (0, 0)
    return pl.pallas_call(
        _ffn_kernel,
        grid=(t // tm, D_FF // tf),
        in_specs=[
            pl.BlockSpec((tm, d), lambda i, f: (i, 0)),
            pl.BlockSpec((tm, PLE_DIM), lambda i, f: (i, 0)),
            pl.BlockSpec((1, d), const),
            pl.BlockSpec((d, tf), lambda i, f: (0, f)),
            pl.BlockSpec((tf, d), lambda i, f: (f, 0)),
            pl.BlockSpec((1, d), const),
            pl.BlockSpec((PLE_DIM, d), const),
            pl.BlockSpec((d, d), const),
            pl.BlockSpec((1, d), const),
        ],
        out_specs=pl.BlockSpec((tm, d), lambda i, f: (i, 0)),
        out_shape=jax.ShapeDtypeStruct((t, d), F32),
        scratch_shapes=[pltpu.VMEM((tm, d), BF16), pltpu.VMEM((tm, d), F32)],
        compiler_params=_compiler_params(("parallel", "arbitrary")),
        name="ffn_ple",
    )(h, p, n_pre.reshape(1, -1), w1, w2, n_post.reshape(1, -1), w_ple, w_pg, n_ple.reshape(1, -1))


def _layer(h, p, norm_mix_pre, w_in, conv_w, conv_b, dt_bias, a_log, d_skip, ssd_norm,
           w_ssd_branch, w_sb_branch, w_gate, b_gate, w_out, norm_mix_post,
           norm_ffn_pre, w_ff1, w_ff2, norm_ffn_post, w_ple, w_ple_gate, norm_ple_post):
    bsz, seqlen, d = h.shape
    t = bsz * seqlen
    x2 = h.reshape(t, d)

    c_dt = D_INNER + D_INNER + 2 * SSD_GROUPS * D_STATE
    c_q = c_dt + SSD_HEADS
    w_dt = jnp.pad(w_in[:, c_dt:c_q], ((0, 0), (0, DT_PAD - SSD_HEADS)))
    w_f32 = jnp.concatenate([w_in[:, :c_dt], w_dt], axis=1).astype(BF16)
    q_scale = LOG2E * SB_HEAD_DIM ** -0.5
    w_qkv = jnp.concatenate([w_in[:, c_q:c_q + SB_WIDTH] * q_scale, w_in[:, c_q + SB_WIDTH:]],
                            axis=1).astype(BF16)
    g_pre = norm_mix_pre.reshape(1, d)

    proj = _norm_matmul(x2, g_pre, w_f32, F32, PROJ_TM, PROJ_TN)
    qkv = _norm_matmul(x2, g_pre, w_qkv, BF16, PROJ_TM, QKV_TN)

    p_ssd = _ssd(proj, conv_w, conv_b, dt_bias, a_log, d_skip, ssd_norm,
                 w_ssd_branch.astype(BF16), bsz, seqlen)
    y_sb = _attention(qkv, bsz, seqlen)

    h1 = _merge(x2, p_ssd, y_sb, g_pre, w_gate.astype(BF16), b_gate, w_sb_branch.astype(BF16),
                w_out.astype(BF16), norm_mix_post)
    h3 = _ffn(h1, p.reshape(t, PLE_DIM), norm_ffn_pre, w_ff1.astype(BF16), w_ff2.astype(BF16),
              norm_ffn_post, w_ple.astype(BF16), w_ple_gate.astype(BF16), norm_ple_post)
    return h3.reshape(bsz, seqlen, d)


def kernel(x, p, norm_mix_pre, w_in, conv_w, conv_b, dt_bias, a_log, d_skip, ssd_norm, w_ssd_branch, w_sb_branch, w_gate, b_gate, w_out, norm_mix_post, norm_ffn_pre, w_ff1, w_ff2, norm_ffn_post, w_ple, w_ple_gate, norm_ple_post):
    h = x
    for i in range(p.shape[0]):
        h = _layer(h, p[i], norm_mix_pre[i], w_in[i], conv_w[i], conv_b[i], dt_bias[i], a_log[i],
                   d_skip[i], ssd_norm[i], w_ssd_branch[i], w_sb_branch[i], w_gate[i], b_gate[i],
                   w_out[i], norm_mix_post[i], norm_ffn_pre[i], w_ff1[i], w_ff2[i],
                   norm_ffn_post[i], w_ple[i], w_ple_gate[i], norm_ple_post[i])
    return h
```

```python
import math

import jax
import jax.numpy as jnp
from jax import lax
from jax.experimental import pallas as pl
from jax.experimental.pallas import tpu as pltpu

F32 = jnp.float32
BF16 = jnp.bfloat16

RMS_EPS = 1e-6
LOG2E = math.log2(math.e)

D_MODEL = 1024
D_INNER = 2048
SSD_HEAD_DIM = 64
SSD_HEADS = 32
SSD_GROUPS = 8
HEADS_PER_GROUP = SSD_HEADS // SSD_GROUPS
D_STATE = 128
CONV_K = 4
SSD_CHUNK = 128
SB_HEADS = 16
SB_HEAD_DIM = 64
SB_WIDTH = 1024
D_FF = 4096
PLE_DIM = 256

LANES = 128
SUBLANES = 8
VMEM_LIMIT_BYTES = 56 * 1024 * 1024

PROJ_TM = 1024
QKV_TN = 1536
SSD_STEP_CHUNKS = 2
ATTN_TQ = 256
ATTN_TK = 256
ATTN_PAIRS = 4
MASKED_LOGIT = -1e30
MERGE_TM = 512
FFN_TM = 1024
FFN_TF = 1024


def _compiler_params(semantics):
    return pltpu.CompilerParams(dimension_semantics=semantics, vmem_limit_bytes=VMEM_LIMIT_BYTES)


def _rms_scale(x):
    return lax.rsqrt(jnp.mean(x * x, axis=-1, keepdims=True) + RMS_EPS)


def _sigmoid(x):
    return 1.0 / (1.0 + jnp.exp(-x))


def _neg_abs(x):
    bits = lax.bitcast_convert_type(x, jnp.int32) | jnp.int32(-2 ** 31)
    return lax.bitcast_convert_type(bits, F32)


def _softplus(x):
    return jnp.maximum(x, 0.0) + jnp.log(1.0 + jnp.exp(-jnp.abs(x)))


def _norm_matmul_kernel(x_ref, g_ref, w_ref, o_ref, n_ref):
    @pl.when(pl.program_id(1) == 0)
    def _():
        x = x_ref[...]
        n_ref[...] = (x * _rms_scale(x) * g_ref[...]).astype(BF16)

    o_ref[...] = jnp.dot(n_ref[...], w_ref[...], preferred_element_type=F32).astype(o_ref.dtype)


def _norm_matmul(x, g, w, out_dtype, tm, tn):
    t, d = x.shape
    n = w.shape[1]
    return pl.pallas_call(
        _norm_matmul_kernel,
        grid=(t // tm, n // tn),
        in_specs=[
            pl.BlockSpec((tm, d), lambda i, j: (i, 0)),
            pl.BlockSpec((1, d), lambda i, j: (0, 0)),
            pl.BlockSpec((d, tn), lambda i, j: (0, j)),
        ],
        out_specs=pl.BlockSpec((tm, tn), lambda i, j: (i, j)),
        out_shape=jax.ShapeDtypeStruct((t, n), out_dtype),
        scratch_shapes=[pltpu.VMEM((tm, d), BF16)],
        compiler_params=_compiler_params(("parallel", "arbitrary")),
        name="norm_matmul",
    )(x, g, w)


def _split3(x):
    p1 = x.astype(BF16)
    r1 = x - p1.astype(F32)
    p2 = r1.astype(BF16)
    r2 = r1 - p2.astype(F32)
    return p1, p2, r2.astype(BF16)


def _ssd_kernel(x_ref, g_ref, w_ref, wdt_ref, cw_ref, cb_ref,
                dtb_ref, alog_ref, dskip_ref, normw_ref, wout_ref, o_ref,
                tail_ref, state_ref, y_ref):
    c = pl.program_id(1)
    last = pl.num_programs(1) - 1

    def project_previous():
        o_ref[...] = jnp.dot(y_ref[...], wout_ref[...], preferred_element_type=F32)

    @pl.when(c == 0)
    def _():
        y_ref[...] = jnp.zeros_like(y_ref)
        tail_ref[...] = jnp.zeros_like(tail_ref)
        state_ref[...] = jnp.zeros_like(state_ref)

    @pl.when(c < last)
    def _():
        _ssd_chunks(project_previous, x_ref, g_ref, w_ref, wdt_ref, cw_ref, cb_ref, dtb_ref,
                    alog_ref, dskip_ref, normw_ref, tail_ref, state_ref, y_ref)

    @pl.when(c == last)
    def _():
        project_previous()


def _ssd_chunks(project_previous, x_ref, g_ref, w_ref, wdt_ref, cw_ref, cb_ref, dtb_ref,
                alog_ref, dskip_ref, normw_ref, tail_ref, state_ref, y_ref):
    L = SSD_CHUNK
    rows_per_step = SSD_STEP_CHUNKS * L
    x = x_ref[...]
    n1 = (x * _rms_scale(x) * g_ref[...]).astype(BF16)

    def project(lo, width):
        return jnp.dot(n1, w_ref[:, lo:lo + width], preferred_element_type=F32)

    dt_raw = jnp.dot(n1, wdt_ref[...], preferred_element_type=F32)
    dt_all = _softplus(dt_raw + dtb_ref[...])
    a = -jnp.exp(alog_ref[...])
    row = lax.broadcasted_iota(jnp.int32, (L, L), 0)
    col = lax.broadcasted_iota(jnp.int32, (L, L), 1)
    causal = row >= col
    tril = jnp.where(causal, 1.0, 0.0).astype(BF16)
    first_half = lax.broadcasted_iota(jnp.int32, (L, LANES), 1) < SSD_HEAD_DIM

    chunks = []
    for ck in range(SSD_STEP_CHUNKS):
        dt = dt_all[ck * L:(ck + 1) * L, :]
        p1, p2, p3 = _split3(dt * a)
        a_cs = (jnp.dot(tril, p1, preferred_element_type=F32)
                + jnp.dot(tril, p2, preferred_element_type=F32)
                + jnp.dot(tril, p3, preferred_element_type=F32))
        a_tot = a_cs[L - 1:L, :]
        w_end = dt * jnp.exp(a_tot - a_cs)
        chunks.append((a_cs.T, dt.T, w_end.T))
    project_previous()

    def conv_silu(lo, width):
        cols = slice(lo, lo + width)
        raw = project(D_INNER + lo, width)
        ext = jnp.concatenate([tail_ref[:, cols], raw], axis=0)
        tail_ref[:, cols] = raw[rows_per_step - SUBLANES:, :]
        acc = cw_ref[0:1, cols] * ext
        for k in range(1, CONV_K):
            acc = cw_ref[k:k + 1, cols] * ext + pltpu.roll(acc, 1, 0)
        acc = acc[SUBLANES:, :] + cb_ref[:, cols]
        return acc * _sigmoid(acc)

    def col_bcast(mat_t, h):
        return jnp.broadcast_to(mat_t[h:h + 1, :], (L, L)).T

    x_groups, b_pairs, c_pairs, z_gates = [], [], [], []
    for g in range(SSD_GROUPS):
        x_groups.append(conv_silu(g * 256, 256))
        if g % 2 == 0:
            b_pairs.append(conv_silu(D_INNER + g * D_STATE, 2 * D_STATE).astype(BF16))
            c_pairs.append(conv_silu(D_INNER + (SSD_GROUPS + g) * D_STATE, 2 * D_STATE).astype(BF16))
        z = project(g * 256, 256)
        z_gates.append(z * _sigmoid(z))

    for g in range(SSD_GROUPS):
        gcols = slice(g * 256, (g + 1) * 256)
        x_all, b_pair, c_pair, z_all = x_groups[g], b_pairs[g // 2], c_pairs[g // 2], z_gates[g]
        half = slice((g % 2) * D_STATE, (g % 2 + 1) * D_STATE)

        for ck, (a_cs_t, dt_t, w_end_t) in enumerate(chunks):
            rows = slice(ck * L, (ck + 1) * L)
            xg = x_all[rows, :]
            xg_b = xg.astype(BF16)
            bg_b = b_pair[rows, half]
            cg_b = c_pair[rows, half]
            cb = lax.dot_general(cg_b, bg_b, (((1,), (1,)), ((), ())),
                                 preferred_element_type=F32)
            st = state_ref[g]
            y_off = jnp.dot(cg_b, st.astype(BF16), preferred_element_type=F32)

            y_diag = []
            ea_cols = []
            w_cols = []
            for j in range(HEADS_PER_GROUP):
                h = g * HEADS_PER_GROUP + j
                a_col = col_bcast(a_cs_t, h)
                seg = a_col - a_cs_t[h:h + 1, :]
                decay = jnp.exp(jnp.where(causal, seg, -jnp.inf))
                m = (cb * decay * dt_t[h:h + 1, :]).astype(BF16)
                pair = (j // 2) * LANES
                y_diag.append(jnp.dot(m, xg_b[:, pair:pair + LANES], preferred_element_type=F32))
                ea_cols.append(jnp.exp(a_col))
                w_cols.append(col_bcast(w_end_t, h))
            yd = jnp.concatenate([jnp.where(first_half, y_diag[0], y_diag[1]),
                                  jnp.where(first_half, y_diag[2], y_diag[3])], axis=1)
            ea = jnp.concatenate([jnp.where(first_half, ea_cols[0], ea_cols[1]),
                                  jnp.where(first_half, ea_cols[2], ea_cols[3])], axis=1)
            wf = jnp.concatenate([jnp.where(first_half, w_cols[0], w_cols[1]),
                                  jnp.where(first_half, w_cols[2], w_cols[3])], axis=1)
            y = yd + y_off * ea + xg * dskip_ref[:, gcols]

            upd = lax.dot_general(bg_b, (xg * wf).astype(BF16), (((0,), (0,)), ((), ())),
                                  preferred_element_type=F32)
            state_ref[g] = st * ea[L - 1:L, :] + upd

            y = y * z_all[rows, :]
            y = y * _rms_scale(y) * normw_ref[:, gcols]
            y_ref[rows, gcols] = y.astype(y_ref.dtype)


def _ssd(x, g_pre, w_in, w_dt, conv_w, conv_b, dt_bias, a_log, d_skip, ssd_norm, w_branch, bsz, seqlen):
    L = SSD_STEP_CHUNKS * SSD_CHUNK
    nc = seqlen // L
    steps = nc + 1
    d = x.shape[1]
    pad = LANES - SSD_HEADS
    dtb = jnp.pad(dt_bias, (0, pad)).reshape(1, LANES)
    alog = jnp.pad(a_log, (0, pad)).reshape(1, LANES)
    dskip = jnp.repeat(d_skip, SSD_HEAD_DIM).reshape(1, D_INNER)
    rows = lambda b, c: b * nc + jnp.minimum(c, nc - 1)
    out_rows = lambda b, c: b * nc + jnp.maximum(c - 1, 0)
    const = lambda b, c: (0, 0)
    conv_dim = 2 * D_INNER
    return pl.pallas_call(
        _ssd_kernel,
        grid=(bsz, steps),
        in_specs=[
            pl.BlockSpec((L, d), lambda b, c: (rows(b, c), 0)),
            pl.BlockSpec((1, d), const),
            pl.BlockSpec((d, D_INNER + conv_dim), const),
            pl.BlockSpec((d, LANES), const),
            pl.BlockSpec((CONV_K, conv_dim), const),
            pl.BlockSpec((1, conv_dim), const),
            pl.BlockSpec((1, LANES), const),
            pl.BlockSpec((1, LANES), const),
            pl.BlockSpec((1, D_INNER), const),
            pl.BlockSpec((1, D_INNER), const),
            pl.BlockSpec((D_INNER, D_MODEL), const),
        ],
        out_specs=pl.BlockSpec((L, D_MODEL), lambda b, c: (out_rows(b, c), 0)),
        out_shape=jax.ShapeDtypeStruct((bsz * seqlen, D_MODEL), F32),
        scratch_shapes=[
            pltpu.VMEM((SUBLANES, conv_dim), F32),
            pltpu.VMEM((SSD_GROUPS, D_STATE, HEADS_PER_GROUP * SSD_HEAD_DIM), F32),
            pltpu.VMEM((L, D_INNER), BF16),
        ],
        compiler_params=_compiler_params(("parallel", "arbitrary")),
        name="ssd",
    )(x, g_pre, w_in, w_dt, conv_w, conv_b.reshape(1, -1), dtb, alog, dskip,
      ssd_norm.reshape(1, D_INNER), w_branch)


def _attn_kernel(q_ref, k_ref, v_ref, u_ref, o_ref, z_scr, sp_scr, acc_scr):
    tq, tk = ATTN_TQ, ATTN_TK
    n_heads = 2 * ATTN_PAIRS
    qi = pl.program_id(2)
    first_half = lax.broadcasted_iota(jnp.int32, (tq, LANES), 1) < SB_HEAD_DIM
    q_heads = []
    for p in range(ATTN_PAIRS):
        q2 = q_ref[:, p * LANES:(p + 1) * LANES]
        zero = jnp.zeros_like(q2)
        q_heads += [jnp.where(first_half, q2, zero), jnp.where(first_half, zero, q2)]
    u = u_ref[...]
    strictly_below = (lax.broadcasted_iota(jnp.int32, (tq, tk), 1)
                      < lax.broadcasted_iota(jnp.int32, (tq, tk), 0))

    def logits_head(kb, slot, h, later, masked):
        start = pl.multiple_of(kb * tk, tk)
        p = h // 2
        kt = k_ref[pl.ds(start, tk), p * LANES:(p + 1) * LANES]
        z = lax.dot_general(q_heads[h], kt, (((1,), (1,)), ((), ())),
                            preferred_element_type=F32)
        sp = jnp.maximum(z, 0.0) + jnp.log2(1.0 + jnp.exp2(_neg_abs(z)))
        if masked:
            sp = jnp.where(strictly_below, sp, 0.0)
            z = jnp.where(strictly_below, z, MASKED_LOGIT)
        z_scr[slot, h] = z if later is None else z - later
        sp_scr[slot, h] = sp.astype(BF16)
        sums = jnp.sum(sp, axis=1, keepdims=True)
        return sums if later is None else later + sums

    def suffix_head(slot, h):
        return jnp.dot(sp_scr[slot, h], u, preferred_element_type=F32)

    def value_head(kb, slot, h, csum):
        start = pl.multiple_of(kb * tk, tk)
        p = h // 2
        a = jnp.exp2(z_scr[slot, h] - csum)
        vt = v_ref[pl.ds(start, tk), p * LANES:(p + 1) * LANES]
        return jnp.dot(a.astype(BF16), vt, preferred_element_type=F32)

    def accumulate(p, out_even, out_odd, first):
        pair = jnp.where(first_half, out_even, out_odd)
        acc_scr[p] = pair if first else acc_scr[p] + pair

    def first_step(slot):
        return tuple(logits_head(qi, slot, h, None, True) for h in range(n_heads))

    def step(kb, old_slot, new_slot, later, first=False):
        new_later, csums, outs = [], [], []
        for t in range(n_heads + 1):
            if t < n_heads:
                csums.append(suffix_head(old_slot, t))
                new_later.append(logits_head(kb, new_slot, t, later[t], False))
            if t >= 1:
                outs.append(value_head(kb + 1, old_slot, t - 1, csums[t - 1]))
                if t % 2 == 0:
                    accumulate(t // 2 - 1, outs[t - 2], outs[t - 1], first)
        return tuple(new_later)

    def last_step(slot, first):
        csums = [suffix_head(slot, h) for h in range(n_heads)]
        for p in range(ATTN_PAIRS):
            accumulate(p, value_head(0, slot, 2 * p, csums[2 * p]),
                       value_head(0, slot, 2 * p + 1, csums[2 * p + 1]), first)

    odd = lax.rem(qi, 2)

    @pl.when(qi == 0)
    def _():
        first_step(0)
        last_step(0, True)

    @pl.when(qi > 0)
    def _():
        def odd_start():
            return step(qi - 1, 1, 0, first_step(1), first=True)

        def even_start():
            later = first_step(0)
            later = step(qi - 1, 0, 1, later, first=True)
            return step(qi - 2, 1, 0, later)

        later = lax.cond(odd == 1, odd_start, even_start)
        remaining = qi - 2 + odd

        def body(j, later):
            kb = remaining - 1 - 2 * j
            return step(kb - 1, 1, 0, step(kb, 0, 1, later))

        lax.fori_loop(0, remaining // 2, body, later)
        last_step(0, False)

    for p in range(ATTN_PAIRS):
        o_ref[:, p * LANES:(p + 1) * LANES] = acc_scr[p].astype(o_ref.dtype)


def _attention(qkv, bsz, seqlen):
    tq, tk = ATTN_TQ, ATTN_TK
    nq = seqlen // tq
    w = ATTN_PAIRS * LANES
    steps = SB_WIDTH // w
    u = (jnp.arange(tk)[:, None] >= jnp.arange(tk)[None, :]).astype(BF16)
    return pl.pallas_call(
        _attn_kernel,
        grid=(bsz, steps, nq),
        in_specs=[
            pl.BlockSpec((tq, w), lambda b, p, i: (b * nq + i, p)),
            pl.BlockSpec((seqlen, w), lambda b, p, i: (b, steps + p)),
            pl.BlockSpec((seqlen, w), lambda b, p, i: (b, 2 * steps + p)),
            pl.BlockSpec((tk, tk), lambda b, p, i: (0, 0)),
        ],
        out_specs=pl.BlockSpec((tq, w), lambda b, p, i: (b * nq + i, p)),
        out_shape=jax.ShapeDtypeStruct((bsz * seqlen, SB_WIDTH), BF16),
        scratch_shapes=[
            pltpu.VMEM((2, 2 * ATTN_PAIRS, tq, tk), F32),
            pltpu.VMEM((2, 2 * ATTN_PAIRS, tq, tk), BF16),
            pltpu.VMEM((ATTN_PAIRS, tq, LANES), F32),
        ],
        compiler_params=_compiler_params(("parallel", "parallel", "arbitrary")),
        name="sb_attention",
    )(qkv, qkv, qkv, u)


def _merge_kernel(x_ref, pssd_ref, ysb_ref, npre_ref, wg_ref, bg_ref, wsb_ref, wout_ref,
                  nw_ref, o_ref):
    x = x_ref[...]
    n1 = (x * _rms_scale(x) * npre_ref[...]).astype(BF16)
    gates = _sigmoid(jnp.dot(n1, wg_ref[...], preferred_element_type=F32) + bg_ref[...])
    merged = (gates[:, 0:D_MODEL] * pssd_ref[...]
              + gates[:, D_MODEL:2 * D_MODEL]
              * jnp.dot(ysb_ref[...], wsb_ref[...], preferred_element_type=F32))
    o = jnp.dot(merged.astype(BF16), wout_ref[...], preferred_element_type=F32)
    o_ref[...] = x + o * _rms_scale(o) * nw_ref[...]


def _merge(x, p_ssd, y_sb, norm_pre, w_gate, b_gate, w_sb, w_out, norm_w):
    t, d = x.shape
    tm = MERGE_TM
    const = lambda i: (0, 0)
    return pl.pallas_call(
        _merge_kernel,
        grid=(t // tm,),
        in_specs=[
            pl.BlockSpec((tm, d), lambda i: (i, 0)),
            pl.BlockSpec((tm, d), lambda i: (i, 0)),
            pl.BlockSpec((tm, SB_WIDTH), lambda i: (i, 0)),
            pl.BlockSpec((1, d), const),
            pl.BlockSpec((d, 2 * d), const),
            pl.BlockSpec((1, 2 * d), const),
            pl.BlockSpec((SB_WIDTH, d), const),
            pl.BlockSpec((d, d), const),
            pl.BlockSpec((1, d), const),
        ],
        out_specs=pl.BlockSpec((tm, d), lambda i: (i, 0)),
        out_shape=jax.ShapeDtypeStruct((t, d), F32),
        compiler_params=_compiler_params(("parallel",)),
        name="merge",
    )(x, p_ssd, y_sb, norm_pre, w_gate, b_gate.reshape(1, -1), w_sb, w_out, norm_w.reshape(1, -1))


def _ffn_kernel(h_ref, p_ref, npre_ref, w1_ref, w2_ref, npost_ref, wple_ref, wpg_ref, nple_ref,
                o_ref, n_ref, acc_ref):
    f = pl.program_id(1)

    @pl.when(f == 0)
    def _():
        h = h_ref[...]
        n_ref[...] = (h * _rms_scale(h) * npre_ref[...]).astype(BF16)

    a = jnp.maximum(jnp.dot(n_ref[...], w1_ref[...], preferred_element_type=F32), 0.0)
    part = jnp.dot((a * a).astype(BF16), w2_ref[...], preferred_element_type=F32)

    @pl.when(f == 0)
    def _():
        acc_ref[...] = part

    @pl.when(f > 0)
    def _():
        acc_ref[...] += part

    @pl.when(f == pl.num_programs(1) - 1)
    def _():
        ff = acc_ref[...]
        h2 = h_ref[...] + ff * _rms_scale(ff) * npost_ref[...]
        gate = _sigmoid(jnp.dot(h2.astype(BF16), wpg_ref[...], preferred_element_type=F32))
        pe = gate * jnp.dot(p_ref[...].astype(BF16), wple_ref[...], preferred_element_type=F32)
        o_ref[...] = h2 + pe * _rms_scale(pe) * nple_ref[...]


def _ffn(h, p, n_pre, w1, w2, n_post, w_ple, w_pg, n_ple):
    t, d = h.shape
    tm, tf = FFN_TM, FFN_TF
    const = lambda i, f: (0, 0)
    return pl.pallas_call(
        _ffn_kernel,
        grid=(t // tm, D_FF // tf),
        in_specs=[
            pl.BlockSpec((tm, d), lambda i, f: (i, 0)),
            pl.BlockSpec((tm, PLE_DIM), lambda i, f: (i, 0)),
            pl.BlockSpec((1, d), const),
            pl.BlockSpec((d, tf), lambda i, f: (0, f)),
            pl.BlockSpec((tf, d), lambda i, f: (f, 0)),
            pl.BlockSpec((1, d), const),
            pl.BlockSpec((PLE_DIM, d), const),
            pl.BlockSpec((d, d), const),
            pl.BlockSpec((1, d), const),
        ],
        out_specs=pl.BlockSpec((tm, d), lambda i, f: (i, 0)),
        out_shape=jax.ShapeDtypeStruct((t, d), F32),
        scratch_shapes=[pltpu.VMEM((tm, d), BF16), pltpu.VMEM((tm, d), F32)],
        compiler_params=_compiler_params(("parallel", "arbitrary")),
        name="ffn_ple",
    )(h, p, n_pre.reshape(1, -1), w1, w2, n_post.reshape(1, -1), w_ple, w_pg, n_ple.reshape(1, -1))


def _layer(h, p, norm_mix_pre, w_in, conv_w, conv_b, dt_bias, a_log, d_skip, ssd_norm,
           w_ssd_branch, w_sb_branch, w_gate, b_gate, w_out, norm_mix_post,
           norm_ffn_pre, w_ff1, w_ff2, norm_ffn_post, w_ple, w_ple_gate, norm_ple_post):
    bsz, seqlen, d = h.shape
    t = bsz * seqlen
    x2 = h.reshape(t, d)

    c_dt = D_INNER + D_INNER + 2 * SSD_GROUPS * D_STATE
    c_q = c_dt + SSD_HEADS
    w_in_b = w_in.astype(BF16)
    w_dt = jnp.pad(w_in_b[:, c_dt:c_q], ((0, 0), (0, LANES - SSD_HEADS)))
    q_scale = LOG2E * SB_HEAD_DIM ** -0.5
    w_qkv = jnp.concatenate([(w_in[:, c_q:c_q + SB_WIDTH] * q_scale).astype(BF16),
                             w_in_b[:, c_q + SB_WIDTH:]], axis=1)
    g_pre = norm_mix_pre.reshape(1, d)

    qkv = _norm_matmul(x2, g_pre, w_qkv, BF16, PROJ_TM, QKV_TN)

    p_ssd = _ssd(x2, g_pre, w_in_b, w_dt, conv_w, conv_b, dt_bias, a_log, d_skip, ssd_norm,
                 w_ssd_branch.astype(BF16), bsz, seqlen)
    y_sb = _attention(qkv, bsz, seqlen)

    h1 = _merge(x2, p_ssd, y_sb, g_pre, w_gate.astype(BF16), b_gate, w_sb_branch.astype(BF16),
                w_out.astype(BF16), norm_mix_post)
    h3 = _ffn(h1, p.reshape(t, PLE_DIM), norm_ffn_pre, w_ff1.astype(BF16), w_ff2.astype(BF16),
              norm_ffn_post, w_ple.astype(BF16), w_ple_gate.astype(BF16), norm_ple_post)
    return h3.reshape(bsz, seqlen, d)


def kernel(x, p, norm_mix_pre, w_in, conv_w, conv_b, dt_bias, a_log, d_skip, ssd_norm, w_ssd_branch, w_sb_branch, w_gate, b_gate, w_out, norm_mix_post, norm_ffn_pre, w_ff1, w_ff2, norm_ffn_post, w_ple, w_ple_gate, norm_ple_post):
    h = x
    for i in range(p.shape[0]):
        h = _layer(h, p[i], norm_mix_pre[i], w_in[i], conv_w[i], conv_b[i], dt_bias[i], a_log[i],
                   d_skip[i], ssd_norm[i], w_ssd_branch[i], w_sb_branch[i], w_gate[i], b_gate[i],
                   w_out[i], norm_mix_post[i], norm_ffn_pre[i], w_ff1[i], w_ff2[i],
                   norm_ffn_post[i], w_ple[i], w_ple_gate[i], norm_ple_post[i])
    return h
```

```python
import math

import jax
import jax.numpy as jnp
from jax import lax
from jax.experimental import pallas as pl
from jax.experimental.pallas import tpu as pltpu

F32 = jnp.float32
BF16 = jnp.bfloat16

RMS_EPS = 1e-6
LOG2E = math.log2(math.e)

D_MODEL = 1024
D_INNER = 2048
SSD_HEAD_DIM = 64
SSD_HEADS = 32
SSD_GROUPS = 8
HEADS_PER_GROUP = SSD_HEADS // SSD_GROUPS
D_STATE = 128
CONV_K = 4
SSD_CHUNK = 128
SB_HEADS = 16
SB_HEAD_DIM = 64
SB_WIDTH = 1024
D_FF = 4096
PLE_DIM = 256

LANES = 128
SUBLANES = 8
VMEM_LIMIT_BYTES = 56 * 1024 * 1024

PROJ_TM = 1024
QKV_TN = 1536
SSD_STEP_CHUNKS = 2
ATTN_TQ = 256
ATTN_TK = 256
ATTN_PAIRS = 4
MASKED_LOGIT = -1e30
MERGE_TM = 512
MERGE_ROW_BLOCKS = 2
FFN_TM = 1024
FFN_TF = 1024
FFN_ROW_BLOCKS = 2


def _compiler_params(semantics):
    return pltpu.CompilerParams(dimension_semantics=semantics, vmem_limit_bytes=VMEM_LIMIT_BYTES)


def _rms_scale(x):
    return lax.rsqrt(jnp.mean(x * x, axis=-1, keepdims=True) + RMS_EPS)


def _sigmoid(x):
    return 1.0 / (1.0 + jnp.exp(-x))


def _neg_abs(x):
    bits = lax.bitcast_convert_type(x, jnp.int32) | jnp.int32(-2 ** 31)
    return lax.bitcast_convert_type(bits, F32)


def _softplus(x):
    return jnp.maximum(x, 0.0) + jnp.log(1.0 + jnp.exp(-jnp.abs(x)))


def _norm_matmul_kernel(x_ref, g_ref, w_ref, o_ref, n_ref):
    @pl.when(pl.program_id(1) == 0)
    def _():
        x = x_ref[...]
        n_ref[...] = (x * _rms_scale(x) * g_ref[...]).astype(BF16)

    o_ref[...] = jnp.dot(n_ref[...], w_ref[...], preferred_element_type=F32).astype(o_ref.dtype)


def _norm_matmul(x, g, w, out_dtype, tm, tn):
    t, d = x.shape
    n = w.shape[1]
    return pl.pallas_call(
        _norm_matmul_kernel,
        grid=(t // tm, n // tn),
        in_specs=[
            pl.BlockSpec((tm, d), lambda i, j: (i, 0)),
            pl.BlockSpec((1, d), lambda i, j: (0, 0)),
            pl.BlockSpec((d, tn), lambda i, j: (0, j)),
        ],
        out_specs=pl.BlockSpec((tm, tn), lambda i, j: (i, j)),
        out_shape=jax.ShapeDtypeStruct((t, n), out_dtype),
        scratch_shapes=[pltpu.VMEM((tm, d), BF16)],
        compiler_params=_compiler_params(("parallel", "arbitrary")),
        name="norm_matmul",
    )(x, g, w)


def _split3(x):
    p1 = x.astype(BF16)
    r1 = x - p1.astype(F32)
    p2 = r1.astype(BF16)
    r2 = r1 - p2.astype(F32)
    return p1, p2, r2.astype(BF16)


def _ssd_kernel(x_ref, g_ref, w_ref, wdt_ref, cw_ref, cb_ref,
                dtb_ref, alog_ref, dskip_ref, normw_ref, wout_ref, o_ref,
                tail_ref, state_ref, y_ref):
    c = pl.program_id(1)
    last = pl.num_programs(1) - 1

    def project_previous():
        o_ref[...] = jnp.dot(y_ref[...], wout_ref[...], preferred_element_type=F32)

    @pl.when(c == 0)
    def _():
        y_ref[...] = jnp.zeros_like(y_ref)
        tail_ref[...] = jnp.zeros_like(tail_ref)
        state_ref[...] = jnp.zeros_like(state_ref)

    @pl.when(c < last)
    def _():
        _ssd_chunks(project_previous, x_ref, g_ref, w_ref, wdt_ref, cw_ref, cb_ref, dtb_ref,
                    alog_ref, dskip_ref, normw_ref, tail_ref, state_ref, y_ref)

    @pl.when(c == last)
    def _():
        project_previous()


def _ssd_chunks(project_previous, x_ref, g_ref, w_ref, wdt_ref, cw_ref, cb_ref, dtb_ref,
                alog_ref, dskip_ref, normw_ref, tail_ref, state_ref, y_ref):
    L = SSD_CHUNK
    rows_per_step = SSD_STEP_CHUNKS * L
    x = x_ref[...]
    n1 = (x * _rms_scale(x) * g_ref[...]).astype(BF16)

    def project(lo, width):
        return jnp.dot(n1, w_ref[:, lo:lo + width], preferred_element_type=F32)

    dt_raw = jnp.dot(n1, wdt_ref[...], preferred_element_type=F32)
    dt_all = _softplus(dt_raw + dtb_ref[...])
    a = -jnp.exp(alog_ref[...])
    row = lax.broadcasted_iota(jnp.int32, (L, L), 0)
    col = lax.broadcasted_iota(jnp.int32, (L, L), 1)
    causal = row >= col
    tril = jnp.where(causal, 1.0, 0.0).astype(BF16)
    first_half = lax.broadcasted_iota(jnp.int32, (L, LANES), 1) < SSD_HEAD_DIM

    chunks = []
    for ck in range(SSD_STEP_CHUNKS):
        dt = dt_all[ck * L:(ck + 1) * L, :]
        p1, p2, p3 = _split3(dt * a)
        a_cs = (jnp.dot(tril, p1, preferred_element_type=F32)
                + jnp.dot(tril, p2, preferred_element_type=F32)
                + jnp.dot(tril, p3, preferred_element_type=F32))
        a_tot = a_cs[L - 1:L, :]
        w_end = dt * jnp.exp(a_tot - a_cs)
        chunks.append((a_cs.T, dt.T, w_end.T))
    project_previous()

    def conv_silu(lo, width):
        cols = slice(lo, lo + width)
        raw = project(D_INNER + lo, width)
        ext = jnp.concatenate([tail_ref[:, cols], raw], axis=0)
        tail_ref[:, cols] = raw[rows_per_step - SUBLANES:, :]
        acc = cw_ref[0:1, cols] * ext
        for k in range(1, CONV_K):
            acc = cw_ref[k:k + 1, cols] * ext + pltpu.roll(acc, 1, 0)
        acc = acc[SUBLANES:, :] + cb_ref[:, cols]
        return acc * _sigmoid(acc)

    def col_bcast(mat_t, h):
        return jnp.broadcast_to(mat_t[h:h + 1, :], (L, L)).T

    x_groups, b_pairs, c_pairs, z_gates = [], [], [], []
    for g in range(SSD_GROUPS):
        x_groups.append(conv_silu(g * 256, 256))
        if g % 2 == 0:
            b_pairs.append(conv_silu(D_INNER + g * D_STATE, 2 * D_STATE).astype(BF16))
            c_pairs.append(conv_silu(D_INNER + (SSD_GROUPS + g) * D_STATE, 2 * D_STATE).astype(BF16))
        z = project(g * 256, 256)
        z_gates.append(z * _sigmoid(z))

    for g in range(SSD_GROUPS):
        gcols = slice(g * 256, (g + 1) * 256)
        x_all, b_pair, c_pair, z_all = x_groups[g], b_pairs[g // 2], c_pairs[g // 2], z_gates[g]
        half = slice((g % 2) * D_STATE, (g % 2 + 1) * D_STATE)

        for ck, (a_cs_t, dt_t, w_end_t) in enumerate(chunks):
            rows = slice(ck * L, (ck + 1) * L)
            xg = x_all[rows, :]
            xg_b = xg.astype(BF16)
            bg_b = b_pair[rows, half]
            cg_b = c_pair[rows, half]
            cb = lax.dot_general(cg_b, bg_b, (((1,), (1,)), ((), ())),
                                 preferred_element_type=F32)
            st = state_ref[g]
            y_off = jnp.dot(cg_b, st.astype(BF16), preferred_element_type=F32)

            y_diag = []
            ea_cols = []
            w_cols = []
            for j in range(HEADS_PER_GROUP):
                h = g * HEADS_PER_GROUP + j
                a_col = col_bcast(a_cs_t, h)
                seg = a_col - a_cs_t[h:h + 1, :]
                decay = jnp.exp(jnp.where(causal, seg, -jnp.inf))
                m = (cb * decay * dt_t[h:h + 1, :]).astype(BF16)
                pair = (j // 2) * LANES
                y_diag.append(jnp.dot(m, xg_b[:, pair:pair + LANES], preferred_element_type=F32))
                ea_cols.append(jnp.exp(a_col))
                w_cols.append(col_bcast(w_end_t, h))
            yd = jnp.concatenate([jnp.where(first_half, y_diag[0], y_diag[1]),
                                  jnp.where(first_half, y_diag[2], y_diag[3])], axis=1)
            ea = jnp.concatenate([jnp.where(first_half, ea_cols[0], ea_cols[1]),
                                  jnp.where(first_half, ea_cols[2], ea_cols[3])], axis=1)
            wf = jnp.concatenate([jnp.where(first_half, w_cols[0], w_cols[1]),
                                  jnp.where(first_half, w_cols[2], w_cols[3])], axis=1)
            y = yd + y_off * ea + xg * dskip_ref[:, gcols]

            upd = lax.dot_general(bg_b, (xg * wf).astype(BF16), (((0,), (0,)), ((), ())),
                                  preferred_element_type=F32)
            state_ref[g] = st * ea[L - 1:L, :] + upd

            y = y * z_all[rows, :]
            y = y * _rms_scale(y) * normw_ref[:, gcols]
            y_ref[rows, gcols] = y.astype(y_ref.dtype)


def _ssd(x, g_pre, w_in, w_dt, conv_w, conv_b, dt_bias, a_log, d_skip, ssd_norm, w_branch, bsz, seqlen):
    L = SSD_STEP_CHUNKS * SSD_CHUNK
    nc = seqlen // L
    steps = nc + 1
    d = x.shape[1]
    pad = LANES - SSD_HEADS
    dtb = jnp.pad(dt_bias, (0, pad)).reshape(1, LANES)
    alog = jnp.pad(a_log, (0, pad)).reshape(1, LANES)
    dskip = jnp.repeat(d_skip, SSD_HEAD_DIM).reshape(1, D_INNER)
    rows = lambda b, c: b * nc + jnp.minimum(c, nc - 1)
    out_rows = lambda b, c: b * nc + jnp.maximum(c - 1, 0)
    const = lambda b, c: (0, 0)
    conv_dim = 2 * D_INNER
    return pl.pallas_call(
        _ssd_kernel,
        grid=(bsz, steps),
        in_specs=[
            pl.BlockSpec((L, d), lambda b, c: (rows(b, c), 0)),
            pl.BlockSpec((1, d), const),
            pl.BlockSpec((d, D_INNER + conv_dim), const),
            pl.BlockSpec((d, LANES), const),
            pl.BlockSpec((CONV_K, conv_dim), const),
            pl.BlockSpec((1, conv_dim), const),
            pl.BlockSpec((1, LANES), const),
            pl.BlockSpec((1, LANES), const),
            pl.BlockSpec((1, D_INNER), const),
            pl.BlockSpec((1, D_INNER), const),
            pl.BlockSpec((D_INNER, D_MODEL), const),
        ],
        out_specs=pl.BlockSpec((L, D_MODEL), lambda b, c: (out_rows(b, c), 0)),
        out_shape=jax.ShapeDtypeStruct((bsz * seqlen, D_MODEL), F32),
        scratch_shapes=[
            pltpu.VMEM((SUBLANES, conv_dim), F32),
            pltpu.VMEM((SSD_GROUPS, D_STATE, HEADS_PER_GROUP * SSD_HEAD_DIM), F32),
            pltpu.VMEM((L, D_INNER), BF16),
        ],
        compiler_params=_compiler_params(("parallel", "arbitrary")),
        name="ssd",
    )(x, g_pre, w_in, w_dt, conv_w, conv_b.reshape(1, -1), dtb, alog, dskip,
      ssd_norm.reshape(1, D_INNER), w_branch)


def _attn_kernel(q_ref, k_ref, v_ref, u_ref, o_ref, z_scr, sp_scr, acc_scr):
    tq, tk = ATTN_TQ, ATTN_TK
    n_heads = 2 * ATTN_PAIRS
    qi = pl.program_id(2)
    first_half = lax.broadcasted_iota(jnp.int32, (tq, LANES), 1) < SB_HEAD_DIM
    q_heads = []
    for p in range(ATTN_PAIRS):
        q2 = q_ref[:, p * LANES:(p + 1) * LANES]
        zero = jnp.zeros_like(q2)
        q_heads += [jnp.where(first_half, q2, zero), jnp.where(first_half, zero, q2)]
    u = u_ref[...]
    strictly_below = (lax.broadcasted_iota(jnp.int32, (tq, tk), 1)
                      < lax.broadcasted_iota(jnp.int32, (tq, tk), 0))

    def logits_head(kb, slot, h, later, masked):
        start = pl.multiple_of(kb * tk, tk)
        p = h // 2
        kt = k_ref[pl.ds(start, tk), p * LANES:(p + 1) * LANES]
        z = lax.dot_general(q_heads[h], kt, (((1,), (1,)), ((), ())),
                            preferred_element_type=F32)
        sp = jnp.maximum(z, 0.0) + jnp.log2(1.0 + jnp.exp2(_neg_abs(z)))
        if masked:
            sp = jnp.where(strictly_below, sp, 0.0)
            z = jnp.where(strictly_below, z, MASKED_LOGIT)
        z_scr[slot, h] = z if later is None else z - later
        sp_scr[slot, h] = sp.astype(BF16)
        sums = jnp.sum(sp, axis=1, keepdims=True)
        return sums if later is None else later + sums

    def suffix_head(slot, h):
        return jnp.dot(sp_scr[slot, h], u, preferred_element_type=F32)

    def value_head(kb, slot, h, csum):
        start = pl.multiple_of(kb * tk, tk)
        p = h // 2
        a = jnp.exp2(z_scr[slot, h] - csum)
        vt = v_ref[pl.ds(start, tk), p * LANES:(p + 1) * LANES]
        return jnp.dot(a.astype(BF16), vt, preferred_element_type=F32)

    def accumulate(p, out_even, out_odd, first):
        pair = jnp.where(first_half, out_even, out_odd)
        acc_scr[p] = pair if first else acc_scr[p] + pair

    def first_step(slot):
        return tuple(logits_head(qi, slot, h, None, True) for h in range(n_heads))

    def step(kb, old_slot, new_slot, later, first=False):
        new_later, csums, outs = [], [], []
        for t in range(n_heads + 1):
            if t < n_heads:
                csums.append(suffix_head(old_slot, t))
                new_later.append(logits_head(kb, new_slot, t, later[t], False))
            if t >= 1:
                outs.append(value_head(kb + 1, old_slot, t - 1, csums[t - 1]))
                if t % 2 == 0:
                    accumulate(t // 2 - 1, outs[t - 2], outs[t - 1], first)
        return tuple(new_later)

    def last_step(slot, first):
        csums = [suffix_head(slot, h) for h in range(n_heads)]
        for p in range(ATTN_PAIRS):
            accumulate(p, value_head(0, slot, 2 * p, csums[2 * p]),
                       value_head(0, slot, 2 * p + 1, csums[2 * p + 1]), first)

    odd = lax.rem(qi, 2)

    @pl.when(qi == 0)
    def _():
        first_step(0)
        last_step(0, True)

    @pl.when(qi > 0)
    def _():
        def odd_start():
            return step(qi - 1, 1, 0, first_step(1), first=True)

        def even_start():
            later = first_step(0)
            later = step(qi - 1, 0, 1, later, first=True)
            return step(qi - 2, 1, 0, later)

        later = lax.cond(odd == 1, odd_start, even_start)
        remaining = qi - 2 + odd

        def body(j, later):
            kb = remaining - 1 - 2 * j
            return step(kb - 1, 1, 0, step(kb, 0, 1, later))

        lax.fori_loop(0, remaining // 2, body, later)
        last_step(0, False)

    for p in range(ATTN_PAIRS):
        o_ref[:, p * LANES:(p + 1) * LANES] = acc_scr[p].astype(o_ref.dtype)


def _attention(qkv, bsz, seqlen):
    tq, tk = ATTN_TQ, ATTN_TK
    nq = seqlen // tq
    w = ATTN_PAIRS * LANES
    steps = SB_WIDTH // w
    u = (jnp.arange(tk)[:, None] >= jnp.arange(tk)[None, :]).astype(BF16)
    return pl.pallas_call(
        _attn_kernel,
        grid=(bsz, steps, nq),
        in_specs=[
            pl.BlockSpec((tq, w), lambda b, p, i: (b * nq + i, p)),
            pl.BlockSpec((seqlen, w), lambda b, p, i: (b, steps + p)),
            pl.BlockSpec((seqlen, w), lambda b, p, i: (b, 2 * steps + p)),
            pl.BlockSpec((tk, tk), lambda b, p, i: (0, 0)),
        ],
        out_specs=pl.BlockSpec((tq, w), lambda b, p, i: (b * nq + i, p)),
        out_shape=jax.ShapeDtypeStruct((bsz * seqlen, SB_WIDTH), BF16),
        scratch_shapes=[
            pltpu.VMEM((2, 2 * ATTN_PAIRS, tq, tk), F32),
            pltpu.VMEM((2, 2 * ATTN_PAIRS, tq, tk), BF16),
            pltpu.VMEM((ATTN_PAIRS, tq, LANES), F32),
        ],
        compiler_params=_compiler_params(("parallel", "parallel", "arbitrary")),
        name="sb_attention",
    )(qkv, qkv, qkv, u)


def _merge_kernel(x_ref, pssd_ref, ysb_ref, npre_ref, wg_ref, bg_ref, wsb_ref, wout_ref,
                  nw_ref, o_ref):
    for r in range(MERGE_ROW_BLOCKS):
        rows = slice(r * MERGE_TM // MERGE_ROW_BLOCKS, (r + 1) * MERGE_TM // MERGE_ROW_BLOCKS)
        x = x_ref[rows, :]
        n1 = (x * _rms_scale(x) * npre_ref[...]).astype(BF16)
        gates = _sigmoid(jnp.dot(n1, wg_ref[...], preferred_element_type=F32) + bg_ref[...])
        merged = (gates[:, 0:D_MODEL] * pssd_ref[rows, :]
                  + gates[:, D_MODEL:2 * D_MODEL]
                  * jnp.dot(ysb_ref[rows, :], wsb_ref[...], preferred_element_type=F32))
        o = jnp.dot(merged.astype(BF16), wout_ref[...], preferred_element_type=F32)
        o_ref[rows, :] = x + o * _rms_scale(o) * nw_ref[...]


def _merge(x, p_ssd, y_sb, norm_pre, w_gate, b_gate, w_sb, w_out, norm_w):
    t, d = x.shape
    tm = MERGE_TM
    const = lambda i: (0, 0)
    return pl.pallas_call(
        _merge_kernel,
        grid=(t // tm,),
        in_specs=[
            pl.BlockSpec((tm, d), lambda i: (i, 0)),
            pl.BlockSpec((tm, d), lambda i: (i, 0)),
            pl.BlockSpec((tm, SB_WIDTH), lambda i: (i, 0)),
            pl.BlockSpec((1, d), const),
            pl.BlockSpec((d, 2 * d), const),
            pl.BlockSpec((1, 2 * d), const),
            pl.BlockSpec((SB_WIDTH, d), const),
            pl.BlockSpec((d, d), const),
            pl.BlockSpec((1, d), const),
        ],
        out_specs=pl.BlockSpec((tm, d), lambda i: (i, 0)),
        out_shape=jax.ShapeDtypeStruct((t, d), F32),
        compiler_params=_compiler_params(("parallel",)),
        name="merge",
    )(x, p_ssd, y_sb, norm_pre, w_gate, b_gate.reshape(1, -1), w_sb, w_out, norm_w.reshape(1, -1))


def _ffn_kernel(h_ref, p_ref, npre_ref, w1_ref, w2_ref, npost_ref, wple_ref, wpg_ref, nple_ref,
                o_ref, n_ref, acc_ref):
    f = pl.program_id(1)

    @pl.when(f == 0)
    def _():
        h = h_ref[...]
        n_ref[...] = (h * _rms_scale(h) * npre_ref[...]).astype(BF16)
        acc_ref[...] = jnp.zeros_like(acc_ref)

    row_blocks = [slice(r * FFN_TM // FFN_ROW_BLOCKS, (r + 1) * FFN_TM // FFN_ROW_BLOCKS)
                  for r in range(FFN_ROW_BLOCKS)]
    for rows in row_blocks:
        a = jnp.maximum(jnp.dot(n_ref[rows, :], w1_ref[...], preferred_element_type=F32), 0.0)
        acc_ref[rows, :] += jnp.dot((a * a).astype(BF16), w2_ref[...], preferred_element_type=F32)

    @pl.when(f == pl.num_programs(1) - 1)
    def _():
        for rows in row_blocks:
            ff = acc_ref[rows, :]
            h2 = h_ref[rows, :] + ff * _rms_scale(ff) * npost_ref[...]
            gate = _sigmoid(jnp.dot(h2.astype(BF16), wpg_ref[...], preferred_element_type=F32))
            pe = gate * jnp.dot(p_ref[rows, :].astype(BF16), wple_ref[...],
                                preferred_element_type=F32)
            o_ref[rows, :] = h2 + pe * _rms_scale(pe) * nple_ref[...]


def _ffn(h, p, n_pre, w1, w2, n_post, w_ple, w_pg, n_ple):
    t, d = h.shape
    tm, tf = FFN_TM, FFN_TF
    const = lambda i, f: (0, 0)
    return pl.pallas_call(
        _ffn_kernel,
        grid=(t // tm, D_FF // tf),
        in_specs=[
            pl.BlockSpec((tm, d), lambda i, f: (i, 0)),
            pl.BlockSpec((tm, PLE_DIM), lambda i, f: (i, 0)),
            pl.BlockSpec((1, d), const),
            pl.BlockSpec((d, tf), lambda i, f: (0, f)),
            pl.BlockSpec((tf, d), lambda i, f: (f, 0)),
            pl.BlockSpec((1, d), const),
            pl.BlockSpec((PLE_DIM, d), const),
            pl.BlockSpec((d, d), const),
            pl.BlockSpec((1, d), const),
        ],
        out_specs=pl.BlockSpec((tm, d), lambda i, f: (i, 0)),
        out_shape=jax.ShapeDtypeStruct((t, d), F32),
        scratch_shapes=[pltpu.VMEM((tm, d), BF16), pltpu.VMEM((tm, d), F32)],
        compiler_params=_compiler_params(("parallel", "arbitrary")),
        name="ffn_ple",
    )(h, p, n_pre.reshape(1, -1), w1, w2, n_post.reshape(1, -1), w_ple, w_pg, n_ple.reshape(1, -1))


def _layer(h, p, norm_mix_pre, w_in, conv_w, conv_b, dt_bias, a_log, d_skip, ssd_norm,
           w_ssd_branch, w_sb_branch, w_gate, b_gate, w_out, norm_mix_post,
           norm_ffn_pre, w_ff1, w_ff2, norm_ffn_post, w_ple, w_ple_gate, norm_ple_post):
    bsz, seqlen, d = h.shape
    t = bsz * seqlen
    x2 = h.reshape(t, d)

    c_dt = D_INNER + D_INNER + 2 * SSD_GROUPS * D_STATE
    c_q = c_dt + SSD_HEADS
    w_in_b = w_in.astype(BF16)
    w_dt = jnp.pad(w_in_b[:, c_dt:c_q], ((0, 0), (0, LANES - SSD_HEADS)))
    q_scale = LOG2E * SB_HEAD_DIM ** -0.5
    w_qkv = jnp.concatenate([(w_in[:, c_q:c_q + SB_WIDTH] * q_scale).astype(BF16),
                             w_in_b[:, c_q + SB_WIDTH:]], axis=1)
    g_pre = norm_mix_pre.reshape(1, d)

    qkv = _norm_matmul(x2, g_pre, w_qkv, BF16, PROJ_TM, QKV_TN)

    p_ssd = _ssd(x2, g_pre, w_in_b, w_dt, conv_w, conv_b, dt_bias, a_log, d_skip, ssd_norm,
                 w_ssd_branch.astype(BF16), bsz, seqlen)
    y_sb = _attention(qkv, bsz, seqlen)

    h1 = _merge(x2, p_ssd, y_sb, g_pre, w_gate.astype(BF16), b_gate, w_sb_branch.astype(BF16),
                w_out.astype(BF16), norm_mix_post)
    h3 = _ffn(h1, p.reshape(t, PLE_DIM), norm_ffn_pre, w_ff1.astype(BF16), w_ff2.astype(BF16),
              norm_ffn_post, w_ple.astype(BF16), w_ple_gate.astype(BF16), norm_ple_post)
    return h3.reshape(bsz, seqlen, d)


def kernel(x, p, norm_mix_pre, w_in, conv_w, conv_b, dt_bias, a_log, d_skip, ssd_norm, w_ssd_branch, w_sb_branch, w_gate, b_gate, w_out, norm_mix_post, norm_ffn_pre, w_ff1, w_ff2, norm_ffn_post, w_ple, w_ple_gate, norm_ple_post):
    h = x
    for i in range(p.shape[0]):
        h = _layer(h, p[i], norm_mix_pre[i], w_in[i], conv_w[i], conv_b[i], dt_bias[i], a_log[i],
                   d_skip[i], ssd_norm[i], w_ssd_branch[i], w_sb_branch[i], w_gate[i], b_gate[i],
                   w_out[i], norm_mix_post[i], norm_ffn_pre[i], w_ff1[i], w_ff2[i],
                   norm_ffn_post[i], w_ple[i], w_ple_gate[i], norm_ple_post[i])
    return h
```

```python
import math

import jax
import jax.numpy as jnp
from jax import lax
from jax.experimental import pallas as pl
from jax.experimental.pallas import tpu as pltpu

F32 = jnp.float32
BF16 = jnp.bfloat16

RMS_EPS = 1e-6
LOG2E = math.log2(math.e)

D_MODEL = 1024
D_INNER = 2048
SSD_HEAD_DIM = 64
SSD_HEADS = 32
SSD_GROUPS = 8
HEADS_PER_GROUP = SSD_HEADS // SSD_GROUPS
D_STATE = 128
CONV_K = 4
SSD_CHUNK = 128
SB_HEADS = 16
SB_HEAD_DIM = 64
SB_WIDTH = 1024
D_FF = 4096
PLE_DIM = 256

LANES = 128
SUBLANES = 8
VMEM_LIMIT_BYTES = 56 * 1024 * 1024

PROJ_TM = 1024
QKV_TN = 1536
SSD_STEP_CHUNKS = 2
ATTN_TQ = 256
ATTN_TK = 256
ATTN_PAIRS = 4
MASKED_LOGIT = -1e30
MERGE_TM = 512
MERGE_ROW_BLOCKS = 2
FFN_TM = 1024
FFN_TF = 1024
FFN_ROW_BLOCKS = 2


def _compiler_params(semantics):
    return pltpu.CompilerParams(dimension_semantics=semantics, vmem_limit_bytes=VMEM_LIMIT_BYTES)


def _rms_scale(x):
    return lax.rsqrt(jnp.mean(x * x, axis=-1, keepdims=True) + RMS_EPS)


def _sigmoid(x):
    return 1.0 / (1.0 + jnp.exp(-x))


def _silu(x):
    h = 0.5 * x
    return h + h * jnp.tanh(h)


def _neg_abs(x):
    bits = lax.bitcast_convert_type(x, jnp.int32) | jnp.int32(-2 ** 31)
    return lax.bitcast_convert_type(bits, F32)


def _softplus(x):
    return jnp.maximum(x, 0.0) + jnp.log(1.0 + jnp.exp(-jnp.abs(x)))


def _norm_matmul_kernel(x_ref, g_ref, w_ref, o_ref, n_ref):
    @pl.when(pl.program_id(1) == 0)
    def _():
        x = x_ref[...]
        n_ref[...] = (x * _rms_scale(x) * g_ref[...]).astype(BF16)

    o_ref[...] = jnp.dot(n_ref[...], w_ref[...], preferred_element_type=F32).astype(o_ref.dtype)


def _norm_matmul(x, g, w, out_dtype, tm, tn):
    t, d = x.shape
    n = w.shape[1]
    return pl.pallas_call(
        _norm_matmul_kernel,
        grid=(t // tm, n // tn),
        in_specs=[
            pl.BlockSpec((tm, d), lambda i, j: (i, 0)),
            pl.BlockSpec((1, d), lambda i, j: (0, 0)),
            pl.BlockSpec((d, tn), lambda i, j: (0, j)),
        ],
        out_specs=pl.BlockSpec((tm, tn), lambda i, j: (i, j)),
        out_shape=jax.ShapeDtypeStruct((t, n), out_dtype),
        scratch_shapes=[pltpu.VMEM((tm, d), BF16)],
        compiler_params=_compiler_params(("parallel", "arbitrary")),
        name="norm_matmul",
    )(x, g, w)


def _split3(x):
    p1 = x.astype(BF16)
    r1 = x - p1.astype(F32)
    p2 = r1.astype(BF16)
    r2 = r1 - p2.astype(F32)
    return p1, p2, r2.astype(BF16)


def _ssd_kernel(x_ref, g_ref, w_ref, wdt_ref, cw_ref, cb_ref,
                dtb_ref, alog_ref, dskip_ref, normw_ref, wout_ref, o_ref,
                tail_ref, state_ref, y_ref):
    c = pl.program_id(1)
    last = pl.num_programs(1) - 1

    def project_previous():
        o_ref[...] = jnp.dot(y_ref[...], wout_ref[...], preferred_element_type=F32)

    @pl.when(c == 0)
    def _():
        y_ref[...] = jnp.zeros_like(y_ref)
        tail_ref[...] = jnp.zeros_like(tail_ref)
        state_ref[...] = jnp.zeros_like(state_ref)

    @pl.when(c < last)
    def _():
        _ssd_chunks(project_previous, x_ref, g_ref, w_ref, wdt_ref, cw_ref, cb_ref, dtb_ref,
                    alog_ref, dskip_ref, normw_ref, tail_ref, state_ref, y_ref)

    @pl.when(c == last)
    def _():
        project_previous()


def _ssd_chunks(project_previous, x_ref, g_ref, w_ref, wdt_ref, cw_ref, cb_ref, dtb_ref,
                alog_ref, dskip_ref, normw_ref, tail_ref, state_ref, y_ref):
    L = SSD_CHUNK
    rows_per_step = SSD_STEP_CHUNKS * L
    x = x_ref[...]
    n1 = (x * _rms_scale(x) * g_ref[...]).astype(BF16)

    def project(lo, width):
        return jnp.dot(n1, w_ref[:, lo:lo + width], preferred_element_type=F32)

    dt_raw = jnp.dot(n1, wdt_ref[...], preferred_element_type=F32)
    dt_all = _softplus(dt_raw + dtb_ref[...])
    a = -jnp.exp(alog_ref[...])
    row = lax.broadcasted_iota(jnp.int32, (L, L), 0)
    col = lax.broadcasted_iota(jnp.int32, (L, L), 1)
    causal = row >= col
    tril = jnp.where(causal, 1.0, 0.0).astype(BF16)
    first_half = lax.broadcasted_iota(jnp.int32, (L, LANES), 1) < SSD_HEAD_DIM

    chunks = []
    for ck in range(SSD_STEP_CHUNKS):
        dt = dt_all[ck * L:(ck + 1) * L, :]
        p1, p2, p3 = _split3(dt * a)
        a_cs = (jnp.dot(tril, p1, preferred_element_type=F32)
                + jnp.dot(tril, p2, preferred_element_type=F32)
                + jnp.dot(tril, p3, preferred_element_type=F32))
        a_tot = a_cs[L - 1:L, :]
        w_end = dt * jnp.exp(a_tot - a_cs)
        chunks.append((a_cs.T, (a_cs - jnp.log(dt)).T, w_end.T, jnp.exp(a_cs).T))
    project_previous()

    def conv_silu(lo, width):
        cols = slice(lo, lo + width)
        raw = project(D_INNER + lo, width)
        ext = jnp.concatenate([tail_ref[:, cols], raw], axis=0)
        tail_ref[:, cols] = raw[rows_per_step - SUBLANES:, :]
        acc = cw_ref[0:1, cols] * ext
        for k in range(1, CONV_K):
            acc = cw_ref[k:k + 1, cols] * ext + pltpu.roll(acc, 1, 0)
        acc = acc[SUBLANES:, :] + cb_ref[:, cols]
        return _silu(acc)

    def col_bcast(mat_t, h):
        return jnp.broadcast_to(mat_t[h:h + 1, :], (L, L)).T

    def pair_bcast(mat_t, h):
        halves = [jnp.broadcast_to(mat_t[h + k:h + k + 1, :], (SSD_HEAD_DIM, L)) for k in range(2)]
        return jnp.concatenate(halves, axis=0).T

    def group_bcast(mat_t, g):
        h = g * HEADS_PER_GROUP
        return jnp.concatenate([pair_bcast(mat_t, h), pair_bcast(mat_t, h + 2)], axis=1)

    x_groups, b_pairs, c_pairs, z_gates = [], [], [], []
    for g in range(SSD_GROUPS):
        x_groups.append(conv_silu(g * 256, 256))
        if g % 2 == 0:
            b_pairs.append(conv_silu(D_INNER + g * D_STATE, 2 * D_STATE).astype(BF16))
            c_pairs.append(conv_silu(D_INNER + (SSD_GROUPS + g) * D_STATE, 2 * D_STATE).astype(BF16))
        z = project(g * 256, 256)
        z_gates.append(_silu(z))

    for g in range(SSD_GROUPS):
        gcols = slice(g * 256, (g + 1) * 256)
        x_all, b_pair, c_pair, z_all = x_groups[g], b_pairs[g // 2], c_pairs[g // 2], z_gates[g]
        half = slice((g % 2) * D_STATE, (g % 2 + 1) * D_STATE)

        for ck, (a_cs_t, a_cs_dt_t, w_end_t, ea_t) in enumerate(chunks):
            rows = slice(ck * L, (ck + 1) * L)
            xg = x_all[rows, :]
            xg_b = xg.astype(BF16)
            bg_b = b_pair[rows, half]
            cg_b = c_pair[rows, half]
            cb = lax.dot_general(cg_b, bg_b, (((1,), (1,)), ((), ())),
                                 preferred_element_type=F32)
            st = state_ref[g]
            y_off = jnp.dot(cg_b, st.astype(BF16), preferred_element_type=F32)

            y_pairs = []
            for jp in range(HEADS_PER_GROUP // 2):
                x_pair = xg_b[:, jp * LANES:(jp + 1) * LANES]
                zero = jnp.zeros_like(x_pair)
                x_halves = (jnp.where(first_half, x_pair, zero), jnp.where(first_half, zero, x_pair))
                parts = []
                for k in range(2):
                    h = g * HEADS_PER_GROUP + 2 * jp + k
                    seg = col_bcast(a_cs_t, h) - a_cs_dt_t[h:h + 1, :]
                    m = (cb * jnp.exp(jnp.where(causal, seg, -jnp.inf))).astype(BF16)
                    parts.append(jnp.dot(m, x_halves[k], preferred_element_type=F32))
                y_pairs.append(parts[0] + parts[1])
            yd = jnp.concatenate(y_pairs, axis=1)
            ea = group_bcast(ea_t, g)
            wf = group_bcast(w_end_t, g)
            y = yd + y_off * ea + xg * dskip_ref[:, gcols]

            upd = lax.dot_general(bg_b, (xg * wf).astype(BF16), (((0,), (0,)), ((), ())),
                                  preferred_element_type=F32)
            state_ref[g] = st * ea[L - 1:L, :] + upd

            y = y * z_all[rows, :]
            y = y * _rms_scale(y) * normw_ref[:, gcols]
            y_ref[rows, gcols] = y.astype(y_ref.dtype)


def _ssd(x, g_pre, w_in, w_dt, conv_w, conv_b, dt_bias, a_log, d_skip, ssd_norm, w_branch, bsz, seqlen):
    L = SSD_STEP_CHUNKS * SSD_CHUNK
    nc = seqlen // L
    steps = nc + 1
    d = x.shape[1]
    pad = LANES - SSD_HEADS
    dtb = jnp.pad(dt_bias, (0, pad)).reshape(1, LANES)
    alog = jnp.pad(a_log, (0, pad)).reshape(1, LANES)
    dskip = jnp.repeat(d_skip, SSD_HEAD_DIM).reshape(1, D_INNER)
    rows = lambda b, c: b * nc + jnp.minimum(c, nc - 1)
    out_rows = lambda b, c: b * nc + jnp.maximum(c - 1, 0)
    const = lambda b, c: (0, 0)
    conv_dim = 2 * D_INNER
    return pl.pallas_call(
        _ssd_kernel,
        grid=(bsz, steps),
        in_specs=[
            pl.BlockSpec((L, d), lambda b, c: (rows(b, c), 0)),
            pl.BlockSpec((1, d), const),
            pl.BlockSpec((d, D_INNER + conv_dim), const),
            pl.BlockSpec((d, LANES), const),
            pl.BlockSpec((CONV_K, conv_dim), const),
            pl.BlockSpec((1, conv_dim), const),
            pl.BlockSpec((1, LANES), const),
            pl.BlockSpec((1, LANES), const),
            pl.BlockSpec((1, D_INNER), const),
            pl.BlockSpec((1, D_INNER), const),
            pl.BlockSpec((D_INNER, D_MODEL), const),
        ],
        out_specs=pl.BlockSpec((L, D_MODEL), lambda b, c: (out_rows(b, c), 0)),
        out_shape=jax.ShapeDtypeStruct((bsz * seqlen, D_MODEL), F32),
        scratch_shapes=[
            pltpu.VMEM((SUBLANES, conv_dim), F32),
            pltpu.VMEM((SSD_GROUPS, D_STATE, HEADS_PER_GROUP * SSD_HEAD_DIM), F32),
            pltpu.VMEM((L, D_INNER), BF16),
        ],
        compiler_params=_compiler_params(("parallel", "arbitrary")),
        name="ssd",
    )(x, g_pre, w_in, w_dt, conv_w, conv_b.reshape(1, -1), dtb, alog, dskip,
      ssd_norm.reshape(1, D_INNER), w_branch)


def _attn_kernel(q_ref, k_ref, v_ref, u_ref, o_ref, z_scr, sp_scr, acc_scr):
    tq, tk = ATTN_TQ, ATTN_TK
    n_heads = 2 * ATTN_PAIRS
    qi = pl.program_id(2)
    first_half = lax.broadcasted_iota(jnp.int32, (tq, LANES), 1) < SB_HEAD_DIM
    q_heads = []
    for p in range(ATTN_PAIRS):
        q2 = q_ref[:, p * LANES:(p + 1) * LANES]
        zero = jnp.zeros_like(q2)
        q_heads += [jnp.where(first_half, q2, zero), jnp.where(first_half, zero, q2)]
    u = u_ref[...]
    strictly_below = (lax.broadcasted_iota(jnp.int32, (tq, tk), 1)
                      < lax.broadcasted_iota(jnp.int32, (tq, tk), 0))

    def logits_head(kb, slot, h, later, masked):
        start = pl.multiple_of(kb * tk, tk)
        p = h // 2
        kt = k_ref[pl.ds(start, tk), p * LANES:(p + 1) * LANES]
        z = lax.dot_general(q_heads[h], kt, (((1,), (1,)), ((), ())),
                            preferred_element_type=F32)
        sp = jnp.maximum(z, 0.0) + jnp.log2(1.0 + jnp.exp2(_neg_abs(z)))
        if masked:
            sp = jnp.where(strictly_below, sp, 0.0)
            z = jnp.where(strictly_below, z, MASKED_LOGIT)
        z_scr[slot, h] = z if later is None else z - later
        sp_scr[slot, h] = sp.astype(BF16)
        sums = jnp.sum(sp, axis=1, keepdims=True)
        return sums if later is None else later + sums

    def suffix_head(slot, h):
        return jnp.dot(sp_scr[slot, h], u, preferred_element_type=F32)

    def value_head(kb, slot, h, csum):
        start = pl.multiple_of(kb * tk, tk)
        p = h // 2
        a = jnp.exp2(z_scr[slot, h] - csum)
        vt = v_ref[pl.ds(start, tk), p * LANES:(p + 1) * LANES]
        return jnp.dot(a.astype(BF16), vt, preferred_element_type=F32)

    def accumulate(p, out_even, out_odd, first):
        pair = jnp.where(first_half, out_even, out_odd)
        acc_scr[p] = pair if first else acc_scr[p] + pair

    def first_step(slot):
        return tuple(logits_head(qi, slot, h, None, True) for h in range(n_heads))

    def step(kb, old_slot, new_slot, later, first=False):
        new_later, csums, outs = [], [], []
        for t in range(n_heads + 1):
            if t < n_heads:
                csums.append(suffix_head(old_slot, t))
                new_later.append(logits_head(kb, new_slot, t, later[t], False))
            if t >= 1:
                outs.append(value_head(kb + 1, old_slot, t - 1, csums[t - 1]))
                if t % 2 == 0:
                    accumulate(t // 2 - 1, outs[t - 2], outs[t - 1], first)
        return tuple(new_later)

    def last_step(slot, first):
        csums = [suffix_head(slot, h) for h in range(n_heads)]
        for p in range(ATTN_PAIRS):
            accumulate(p, value_head(0, slot, 2 * p, csums[2 * p]),
                       value_head(0, slot, 2 * p + 1, csums[2 * p + 1]), first)

    odd = lax.rem(qi, 2)

    @pl.when(qi == 0)
    def _():
        first_step(0)
        last_step(0, True)

    @pl.when(qi > 0)
    def _():
        def odd_start():
            return step(qi - 1, 1, 0, first_step(1), first=True)

        def even_start():
            later = first_step(0)
            later = step(qi - 1, 0, 1, later, first=True)
            return step(qi - 2, 1, 0, later)

        later = lax.cond(odd == 1, odd_start, even_start)
        remaining = qi - 2 + odd

        def body(j, later):
            kb = remaining - 1 - 2 * j
            return step(kb - 1, 1, 0, step(kb, 0, 1, later))

        lax.fori_loop(0, remaining // 2, body, later)
        last_step(0, False)

    for p in range(ATTN_PAIRS):
        o_ref[:, p * LANES:(p + 1) * LANES] = acc_scr[p].astype(o_ref.dtype)


def _attention(qkv, bsz, seqlen):
    tq, tk = ATTN_TQ, ATTN_TK
    nq = seqlen // tq
    w = ATTN_PAIRS * LANES
    steps = SB_WIDTH // w
    u = (jnp.arange(tk)[:, None] >= jnp.arange(tk)[None, :]).astype(BF16)
    return pl.pallas_call(
        _attn_kernel,
        grid=(bsz, steps, nq),
        in_specs=[
            pl.BlockSpec((tq, w), lambda b, p, i: (b * nq + i, p)),
            pl.BlockSpec((seqlen, w), lambda b, p, i: (b, steps + p)),
            pl.BlockSpec((seqlen, w), lambda b, p, i: (b, 2 * steps + p)),
            pl.BlockSpec((tk, tk), lambda b, p, i: (0, 0)),
        ],
        out_specs=pl.BlockSpec((tq, w), lambda b, p, i: (b * nq + i, p)),
        out_shape=jax.ShapeDtypeStruct((bsz * seqlen, SB_WIDTH), BF16),
        scratch_shapes=[
            pltpu.VMEM((2, 2 * ATTN_PAIRS, tq, tk), F32),
            pltpu.VMEM((2, 2 * ATTN_PAIRS, tq, tk), BF16),
            pltpu.VMEM((ATTN_PAIRS, tq, LANES), F32),
        ],
        compiler_params=_compiler_params(("parallel", "parallel", "arbitrary")),
        name="sb_attention",
    )(qkv, qkv, qkv, u)


def _merge_kernel(x_ref, pssd_ref, ysb_ref, npre_ref, wg_ref, bg_ref, wsb_ref, wout_ref,
                  nw_ref, o_ref):
    for r in range(MERGE_ROW_BLOCKS):
        rows = slice(r * MERGE_TM // MERGE_ROW_BLOCKS, (r + 1) * MERGE_TM // MERGE_ROW_BLOCKS)
        x = x_ref[rows, :]
        n1 = (x * _rms_scale(x) * npre_ref[...]).astype(BF16)
        gates = _sigmoid(jnp.dot(n1, wg_ref[...], preferred_element_type=F32) + bg_ref[...])
        merged = (gates[:, 0:D_MODEL] * pssd_ref[rows, :]
                  + gates[:, D_MODEL:2 * D_MODEL]
                  * jnp.dot(ysb_ref[rows, :], wsb_ref[...], preferred_element_type=F32))
        o = jnp.dot(merged.astype(BF16), wout_ref[...], preferred_element_type=F32)
        o_ref[rows, :] = x + o * _rms_scale(o) * nw_ref[...]


def _merge(x, p_ssd, y_sb, norm_pre, w_gate, b_gate, w_sb, w_out, norm_w):
    t, d = x.shape
    tm = MERGE_TM
    const = lambda i: (0, 0)
    return pl.pallas_call(
        _merge_kernel,
        grid=(t // tm,),
        in_specs=[
            pl.BlockSpec((tm, d), lambda i: (i, 0)),
            pl.BlockSpec((tm, d), lambda i: (i, 0)),
            pl.BlockSpec((tm, SB_WIDTH), lambda i: (i, 0)),
            pl.BlockSpec((1, d), const),
            pl.BlockSpec((d, 2 * d), const),
            pl.BlockSpec((1, 2 * d), const),
            pl.BlockSpec((SB_WIDTH, d), const),
            pl.BlockSpec((d, d), const),
            pl.BlockSpec((1, d), const),
        ],
        out_specs=pl.BlockSpec((tm, d), lambda i: (i, 0)),
        out_shape=jax.ShapeDtypeStruct((t, d), F32),
        compiler_params=_compiler_params(("parallel",)),
        name="merge",
    )(x, p_ssd, y_sb, norm_pre, w_gate, b_gate.reshape(1, -1), w_sb, w_out, norm_w.reshape(1, -1))


def _ffn_kernel(h_ref, p_ref, npre_ref, w1_ref, w2_ref, npost_ref, wple_ref, wpg_ref, nple_ref,
                o_ref, n_ref, acc_ref):
    f = pl.program_id(1)

    @pl.when(f == 0)
    def _():
        h = h_ref[...]
        n_ref[...] = (h * _rms_scale(h) * npre_ref[...]).astype(BF16)
        acc_ref[...] = jnp.zeros_like(acc_ref)

    row_blocks = [slice(r * FFN_TM // FFN_ROW_BLOCKS, (r + 1) * FFN_TM // FFN_ROW_BLOCKS)
                  for r in range(FFN_ROW_BLOCKS)]
    for rows in row_blocks:
        a = jnp.maximum(jnp.dot(n_ref[rows, :], w1_ref[...], preferred_element_type=F32), 0.0)
        acc_ref[rows, :] += jnp.dot((a * a).astype(BF16), w2_ref[...], preferred_element_type=F32)

    @pl.when(f == pl.num_programs(1) - 1)
    def _():
        for rows in row_blocks:
            ff = acc_ref[rows, :]
            h2 = h_ref[rows, :] + ff * _rms_scale(ff) * npost_ref[...]
            gate = _sigmoid(jnp.dot(h2.astype(BF16), wpg_ref[...], preferred_element_type=F32))
            pe = gate * jnp.dot(p_ref[rows, :].astype(BF16), wple_ref[...],
                                preferred_element_type=F32)
            o_ref[rows, :] = h2 + pe * _rms_scale(pe) * nple_ref[...]


def _ffn(h, p, n_pre, w1, w2, n_post, w_ple, w_pg, n_ple):
    t, d = h.shape
    tm, tf = FFN_TM, FFN_TF
    const = lambda i, f: (0, 0)
    return pl.pallas_call(
        _ffn_kernel,
        grid=(t // tm, D_FF // tf),
        in_specs=[
            pl.BlockSpec((tm, d), lambda i, f: (i, 0)),
            pl.BlockSpec((tm, PLE_DIM), lambda i, f: (i, 0)),
            pl.BlockSpec((1, d), const),
            pl.BlockSpec((d, tf), lambda i, f: (0, f)),
            pl.BlockSpec((tf, d), lambda i, f: (f, 0)),
            pl.BlockSpec((1, d), const),
            pl.BlockSpec((PLE_DIM, d), const),
            pl.BlockSpec((d, d), const),
            pl.BlockSpec((1, d), const),
        ],
        out_specs=pl.BlockSpec((tm, d), lambda i, f: (i, 0)),
        out_shape=jax.ShapeDtypeStruct((t, d), F32),
        scratch_shapes=[pltpu.VMEM((tm, d), BF16), pltpu.VMEM((tm, d), F32)],
        compiler_params=_compiler_params(("parallel", "arbitrary")),
        name="ffn_ple",
    )(h, p, n_pre.reshape(1, -1), w1, w2, n_post.reshape(1, -1), w_ple, w_pg, n_ple.reshape(1, -1))


def _layer(h, p, norm_mix_pre, w_in, conv_w, conv_b, dt_bias, a_log, d_skip, ssd_norm,
           w_ssd_branch, w_sb_branch, w_gate, b_gate, w_out, norm_mix_post,
           norm_ffn_pre, w_ff1, w_ff2, norm_ffn_post, w_ple, w_ple_gate, norm_ple_post):
    bsz, seqlen, d = h.shape
    t = bsz * seqlen
    x2 = h.reshape(t, d)

    c_dt = D_INNER + D_INNER + 2 * SSD_GROUPS * D_STATE
    c_q = c_dt + SSD_HEADS
    w_in_b = w_in.astype(BF16)
    w_dt = jnp.pad(w_in_b[:, c_dt:c_q], ((0, 0), (0, LANES - SSD_HEADS)))
    q_scale = LOG2E * SB_HEAD_DIM ** -0.5
    w_qkv = jnp.concatenate([(w_in[:, c_q:c_q + SB_WIDTH] * q_scale).astype(BF16),
                             w_in_b[:, c_q + SB_WIDTH:]], axis=1)
    g_pre = norm_mix_pre.reshape(1, d)

    qkv = _norm_matmul(x2, g_pre, w_qkv, BF16, PROJ_TM, QKV_TN)

    p_ssd = _ssd(x2, g_pre, w_in_b, w_dt, conv_w, conv_b, dt_bias, a_log, d_skip, ssd_norm,
                 w_ssd_branch.astype(BF16), bsz, seqlen)
    y_sb = _attention(qkv, bsz, seqlen)

    h1 = _merge(x2, p_ssd, y_sb, g_pre, w_gate.astype(BF16), b_gate, w_sb_branch.astype(BF16),
                w_out.astype(BF16), norm_mix_post)
    h3 = _ffn(h1, p.reshape(t, PLE_DIM), norm_ffn_pre, w_ff1.astype(BF16), w_ff2.astype(BF16),
              norm_ffn_post, w_ple.astype(BF16), w_ple_gate.astype(BF16), norm_ple_post)
    return h3.reshape(bsz, seqlen, d)


def kernel(x, p, norm_mix_pre, w_in, conv_w, conv_b, dt_bias, a_log, d_skip, ssd_norm, w_ssd_branch, w_sb_branch, w_gate, b_gate, w_out, norm_mix_post, norm_ffn_pre, w_ff1, w_ff2, norm_ffn_post, w_ple, w_ple_gate, norm_ple_post):
    h = x
    for i in range(p.shape[0]):
        h = _layer(h, p[i], norm_mix_pre[i], w_in[i], conv_w[i], conv_b[i], dt_bias[i], a_log[i],
                   d_skip[i], ssd_norm[i], w_ssd_branch[i], w_sb_branch[i], w_gate[i], b_gate[i],
                   w_out[i], norm_mix_post[i], norm_ffn_pre[i], w_ff1[i], w_ff2[i],
                   norm_ffn_post[i], w_ple[i], w_ple_gate[i], norm_ple_post[i])
    return h
```

```python
import math

import jax
import jax.numpy as jnp
from jax import lax
from jax.experimental import pallas as pl
from jax.experimental.pallas import tpu as pltpu

F32 = jnp.float32
BF16 = jnp.bfloat16

RMS_EPS = 1e-6
LOG2E = math.log2(math.e)

D_MODEL = 1024
D_INNER = 2048
SSD_HEAD_DIM = 64
SSD_HEADS = 32
SSD_GROUPS = 8
HEADS_PER_GROUP = SSD_HEADS // SSD_GROUPS
D_STATE = 128
CONV_K = 4
SSD_CHUNK = 128
SB_HEADS = 16
SB_HEAD_DIM = 64
SB_WIDTH = 1024
D_FF = 4096
PLE_DIM = 256

LANES = 128
SUBLANES = 8
VMEM_LIMIT_BYTES = 56 * 1024 * 1024

PROJ_TM = 1024
QKV_TN = 1536
SSD_STEP_CHUNKS = 2
ATTN_TQ = 256
ATTN_TK = 256
ATTN_PAIRS = 4
MASKED_LOGIT = -1e30
MERGE_TM = 512
MERGE_ROW_BLOCKS = 2
FFN_TM = 1024
FFN_TF = 1024
FFN_ROW_BLOCKS = 2


def _compiler_params(semantics):
    return pltpu.CompilerParams(dimension_semantics=semantics, vmem_limit_bytes=VMEM_LIMIT_BYTES)


def _rms_scale(x):
    return lax.rsqrt(jnp.mean(x * x, axis=-1, keepdims=True) + RMS_EPS)


def _sigmoid(x):
    return 1.0 / (1.0 + jnp.exp(-x))


def _silu(x):
    h = 0.5 * x
    return h + h * jnp.tanh(h)


def _neg_abs(x):
    bits = lax.bitcast_convert_type(x, jnp.int32) | jnp.int32(-2 ** 31)
    return lax.bitcast_convert_type(bits, F32)


def _softplus(x):
    return jnp.maximum(x, 0.0) + jnp.log(1.0 + jnp.exp(-jnp.abs(x)))


def _norm_matmul_kernel(x_ref, g_ref, w_ref, o_ref, n_ref):
    @pl.when(pl.program_id(1) == 0)
    def _():
        x = x_ref[...]
        n_ref[...] = (x * _rms_scale(x) * g_ref[...]).astype(BF16)

    o_ref[...] = jnp.dot(n_ref[...], w_ref[...], preferred_element_type=F32).astype(o_ref.dtype)


def _norm_matmul(x, g, w, out_dtype, tm, tn):
    t, d = x.shape
    n = w.shape[1]
    return pl.pallas_call(
        _norm_matmul_kernel,
        grid=(t // tm, n // tn),
        in_specs=[
            pl.BlockSpec((tm, d), lambda i, j: (i, 0)),
            pl.BlockSpec((1, d), lambda i, j: (0, 0)),
            pl.BlockSpec((d, tn), lambda i, j: (0, j)),
        ],
        out_specs=pl.BlockSpec((tm, tn), lambda i, j: (i, j)),
        out_shape=jax.ShapeDtypeStruct((t, n), out_dtype),
        scratch_shapes=[pltpu.VMEM((tm, d), BF16)],
        compiler_params=_compiler_params(("parallel", "arbitrary")),
        name="norm_matmul",
    )(x, g, w)


def _split3(x):
    p1 = x.astype(BF16)
    r1 = x - p1.astype(F32)
    p2 = r1.astype(BF16)
    r2 = r1 - p2.astype(F32)
    return p1, p2, r2.astype(BF16)


def _ssd_kernel(x_ref, g_ref, w_ref, wdt_ref, cw_ref, cb_ref,
                dtb_ref, alog_ref, dskip_ref, normw_ref, wout_ref, o_ref,
                tail_ref, state_ref, y_ref):
    c = pl.program_id(1)
    last = pl.num_programs(1) - 1

    def project_previous():
        o_ref[...] = jnp.dot(y_ref[...], wout_ref[...], preferred_element_type=F32)

    @pl.when(c == 0)
    def _():
        y_ref[...] = jnp.zeros_like(y_ref)
        tail_ref[...] = jnp.zeros_like(tail_ref)
        state_ref[...] = jnp.zeros_like(state_ref)

    @pl.when(c < last)
    def _():
        _ssd_chunks(project_previous, x_ref, g_ref, w_ref, wdt_ref, cw_ref, cb_ref, dtb_ref,
                    alog_ref, dskip_ref, normw_ref, tail_ref, state_ref, y_ref)

    @pl.when(c == last)
    def _():
        project_previous()


def _ssd_chunks(project_previous, x_ref, g_ref, w_ref, wdt_ref, cw_ref, cb_ref, dtb_ref,
                alog_ref, dskip_ref, normw_ref, tail_ref, state_ref, y_ref):
    L = SSD_CHUNK
    rows_per_step = SSD_STEP_CHUNKS * L
    x = x_ref[...]
    n1 = (x * _rms_scale(x) * g_ref[...]).astype(BF16)

    def project(lo, width):
        return jnp.dot(n1, w_ref[:, lo:lo + width], preferred_element_type=F32)

    dt_raw = jnp.dot(n1, wdt_ref[...], preferred_element_type=F32)
    dt_all = _softplus(dt_raw + dtb_ref[...])
    a = -jnp.exp(alog_ref[...])
    row = lax.broadcasted_iota(jnp.int32, (L, L), 0)
    col = lax.broadcasted_iota(jnp.int32, (L, L), 1)
    causal = row >= col
    tril = jnp.where(causal, 1.0, 0.0).astype(BF16)
    first_half = lax.broadcasted_iota(jnp.int32, (L, LANES), 1) < SSD_HEAD_DIM

    chunks = []
    for ck in range(SSD_STEP_CHUNKS):
        dt = dt_all[ck * L:(ck + 1) * L, :]
        p1, p2, p3 = _split3(dt * a)
        a_cs = (jnp.dot(tril, p1, preferred_element_type=F32)
                + jnp.dot(tril, p2, preferred_element_type=F32)
                + jnp.dot(tril, p3, preferred_element_type=F32))
        a_tot = a_cs[L - 1:L, :]
        w_end = dt * jnp.exp(a_tot - a_cs)
        chunks.append((a_cs.T, (a_cs - jnp.log(dt)).T, w_end.T, jnp.exp(a_cs).T))
    project_previous()

    def conv_silu(lo, width):
        cols = slice(lo, lo + width)
        raw = project(D_INNER + lo, width)
        ext = jnp.concatenate([tail_ref[:, cols], raw], axis=0)
        tail_ref[:, cols] = raw[rows_per_step - SUBLANES:, :]
        acc = cw_ref[0:1, cols] * ext
        for k in range(1, CONV_K):
            acc = cw_ref[k:k + 1, cols] * ext + pltpu.roll(acc, 1, 0)
        acc = acc[SUBLANES:, :] + cb_ref[:, cols]
        return _silu(acc)

    def col_bcast(mat_t, h):
        return jnp.broadcast_to(mat_t[h:h + 1, :], (L, L)).T

    def pair_bcast(mat_t, h):
        halves = [jnp.broadcast_to(mat_t[h + k:h + k + 1, :], (SSD_HEAD_DIM, L)) for k in range(2)]
        return jnp.concatenate(halves, axis=0).T

    def group_bcast(mat_t, g):
        h = g * HEADS_PER_GROUP
        return jnp.concatenate([pair_bcast(mat_t, h), pair_bcast(mat_t, h + 2)], axis=1)

    x_groups, b_pairs, c_pairs, z_gates = [], [], [], []
    for g in range(SSD_GROUPS):
        x_groups.append(conv_silu(g * 256, 256))
        if g % 2 == 0:
            b_pairs.append(conv_silu(D_INNER + g * D_STATE, 2 * D_STATE).astype(BF16))
            c_pairs.append(conv_silu(D_INNER + (SSD_GROUPS + g) * D_STATE, 2 * D_STATE).astype(BF16))
        z = project(g * 256, 256)
        z_gates.append(_silu(z))

    for g in range(SSD_GROUPS):
        gcols = slice(g * 256, (g + 1) * 256)
        x_all, b_pair, c_pair, z_all = x_groups[g], b_pairs[g // 2], c_pairs[g // 2], z_gates[g]
        half = slice((g % 2) * D_STATE, (g % 2 + 1) * D_STATE)

        for ck, (a_cs_t, a_cs_dt_t, w_end_t, ea_t) in enumerate(chunks):
            rows = slice(ck * L, (ck + 1) * L)
            xg = x_all[rows, :]
            xg_b = xg.astype(BF16)
            bg_b = b_pair[rows, half]
            cg_b = c_pair[rows, half]
            cb = lax.dot_general(cg_b, bg_b, (((1,), (1,)), ((), ())),
                                 preferred_element_type=F32)
            st = state_ref[g]
            y_off = jnp.dot(cg_b, st.astype(BF16), preferred_element_type=F32)

            y_pairs = []
            for jp in range(HEADS_PER_GROUP // 2):
                x_pair = xg_b[:, jp * LANES:(jp + 1) * LANES]
                zero = jnp.zeros_like(x_pair)
                x_halves = (jnp.where(first_half, x_pair, zero), jnp.where(first_half, zero, x_pair))
                parts = []
                for k in range(2):
                    h = g * HEADS_PER_GROUP + 2 * jp + k
                    seg = col_bcast(a_cs_t, h) - a_cs_dt_t[h:h + 1, :]
                    m = (cb * jnp.exp(jnp.where(causal, seg, -jnp.inf))).astype(BF16)
                    parts.append(jnp.dot(m, x_halves[k], preferred_element_type=F32))
                y_pairs.append(parts[0] + parts[1])
            yd = jnp.concatenate(y_pairs, axis=1)
            ea = group_bcast(ea_t, g)
            wf = group_bcast(w_end_t, g)
            y = yd + y_off * ea + xg * dskip_ref[:, gcols]

            upd = lax.dot_general(bg_b, (xg * wf).astype(BF16), (((0,), (0,)), ((), ())),
                                  preferred_element_type=F32)
            state_ref[g] = st * ea[L - 1:L, :] + upd

            y = y * z_all[rows, :]
            y = y * _rms_scale(y) * normw_ref[:, gcols]
            y_ref[rows, gcols] = y.astype(y_ref.dtype)


def _ssd(x, g_pre, w_in, w_dt, conv_w, conv_b, dt_bias, a_log, d_skip, ssd_norm, w_branch, bsz, seqlen):
    L = SSD_STEP_CHUNKS * SSD_CHUNK
    nc = seqlen // L
    steps = nc + 1
    d = x.shape[1]
    pad = LANES - SSD_HEADS
    dtb = jnp.pad(dt_bias, (0, pad)).reshape(1, LANES)
    alog = jnp.pad(a_log, (0, pad)).reshape(1, LANES)
    dskip = jnp.repeat(d_skip, SSD_HEAD_DIM).reshape(1, D_INNER)
    rows = lambda b, c: b * nc + jnp.minimum(c, nc - 1)
    out_rows = lambda b, c: b * nc + jnp.maximum(c - 1, 0)
    const = lambda b, c: (0, 0)
    conv_dim = 2 * D_INNER
    return pl.pallas_call(
        _ssd_kernel,
        grid=(bsz, steps),
        in_specs=[
            pl.BlockSpec((L, d), lambda b, c: (rows(b, c), 0)),
            pl.BlockSpec((1, d), const),
            pl.BlockSpec((d, D_INNER + conv_dim), const),
            pl.BlockSpec((d, LANES), const),
            pl.BlockSpec((CONV_K, conv_dim), const),
            pl.BlockSpec((1, conv_dim), const),
            pl.BlockSpec((1, LANES), const),
            pl.BlockSpec((1, LANES), const),
            pl.BlockSpec((1, D_INNER), const),
            pl.BlockSpec((1, D_INNER), const),
            pl.BlockSpec((D_INNER, D_MODEL), const),
        ],
        out_specs=pl.BlockSpec((L, D_MODEL), lambda b, c: (out_rows(b, c), 0)),
        out_shape=jax.ShapeDtypeStruct((bsz * seqlen, D_MODEL), F32),
        scratch_shapes=[
            pltpu.VMEM((SUBLANES, conv_dim), F32),
            pltpu.VMEM((SSD_GROUPS, D_STATE, HEADS_PER_GROUP * SSD_HEAD_DIM), F32),
            pltpu.VMEM((L, D_INNER), BF16),
        ],
        compiler_params=_compiler_params(("parallel", "arbitrary")),
        name="ssd",
    )(x, g_pre, w_in, w_dt, conv_w, conv_b.reshape(1, -1), dtb, alog, dskip,
      ssd_norm.reshape(1, D_INNER), w_branch)


def _attn_kernel(q_ref, k_ref, v_ref, u_ref, o_ref, z_scr, sp_scr, acc_scr):
    tq, tk = ATTN_TQ, ATTN_TK
    n_heads = 2 * ATTN_PAIRS
    qi = pl.program_id(2)
    first_half = lax.broadcasted_iota(jnp.int32, (tq, LANES), 1) < SB_HEAD_DIM
    q_heads = []
    for p in range(ATTN_PAIRS):
        q2 = q_ref[:, p * LANES:(p + 1) * LANES]
        zero = jnp.zeros_like(q2)
        q_heads += [jnp.where(first_half, q2, zero), jnp.where(first_half, zero, q2)]
    u = u_ref[...]
    strictly_below = (lax.broadcasted_iota(jnp.int32, (tq, tk), 1)
                      < lax.broadcasted_iota(jnp.int32, (tq, tk), 0))

    def logits_head(kb, slot, h, later, masked):
        start = pl.multiple_of(kb * tk, tk)
        p = h // 2
        kt = k_ref[pl.ds(start, tk), p * LANES:(p + 1) * LANES]
        z = lax.dot_general(q_heads[h], kt, (((1,), (1,)), ((), ())),
                            preferred_element_type=F32)
        sp = jnp.maximum(z, 0.0) + jnp.log2(1.0 + jnp.exp2(_neg_abs(z)))
        if masked:
            sp = jnp.where(strictly_below, sp, 0.0)
            z = jnp.where(strictly_below, z, MASKED_LOGIT)
        own = z - sp
        z_scr[slot, h] = own if later is None else own - later
        sp_scr[slot, h] = sp.astype(BF16)
        sums = jnp.sum(sp, axis=1, keepdims=True)
        return sums if later is None else later + sums

    def suffix_head(slot, h):
        return jnp.dot(sp_scr[slot, h], u, preferred_element_type=F32)

    def value_head(kb, slot, h, csum):
        start = pl.multiple_of(kb * tk, tk)
        p = h // 2
        a = jnp.exp2(z_scr[slot, h] - csum)
        vt = v_ref[pl.ds(start, tk), p * LANES:(p + 1) * LANES]
        return jnp.dot(a.astype(BF16), vt, preferred_element_type=F32)

    def accumulate(p, out_even, out_odd, first):
        pair = jnp.where(first_half, out_even, out_odd)
        acc_scr[p] = pair if first else acc_scr[p] + pair

    def first_step(slot):
        return tuple(logits_head(qi, slot, h, None, True) for h in range(n_heads))

    def step(kb, old_slot, new_slot, later, first=False):
        new_later, csums, outs = [], [], []
        for t in range(n_heads + 1):
            if t < n_heads:
                csums.append(suffix_head(old_slot, t))
                new_later.append(logits_head(kb, new_slot, t, later[t], False))
            if t >= 1:
                outs.append(value_head(kb + 1, old_slot, t - 1, csums[t - 1]))
                if t % 2 == 0:
                    accumulate(t // 2 - 1, outs[t - 2], outs[t - 1], first)
        return tuple(new_later)

    def last_step(slot, first):
        csums = [suffix_head(slot, h) for h in range(n_heads)]
        for p in range(ATTN_PAIRS):
            accumulate(p, value_head(0, slot, 2 * p, csums[2 * p]),
                       value_head(0, slot, 2 * p + 1, csums[2 * p + 1]), first)

    odd = lax.rem(qi, 2)

    @pl.when(qi == 0)
    def _():
        first_step(0)
        last_step(0, True)

    @pl.when(qi > 0)
    def _():
        def odd_start():
            return step(qi - 1, 1, 0, first_step(1), first=True)

        def even_start():
            later = first_step(0)
            later = step(qi - 1, 0, 1, later, first=True)
            return step(qi - 2, 1, 0, later)

        later = lax.cond(odd == 1, odd_start, even_start)
        remaining = qi - 2 + odd

        def body(j, later):
            kb = remaining - 1 - 2 * j
            return step(kb - 1, 1, 0, step(kb, 0, 1, later))

        lax.fori_loop(0, remaining // 2, body, later)
        last_step(0, False)

    for p in range(ATTN_PAIRS):
        o_ref[:, p * LANES:(p + 1) * LANES] = acc_scr[p].astype(o_ref.dtype)


def _attention(qkv, bsz, seqlen):
    tq, tk = ATTN_TQ, ATTN_TK
    nq = seqlen // tq
    w = ATTN_PAIRS * LANES
    steps = SB_WIDTH // w
    u = (jnp.arange(tk)[:, None] > jnp.arange(tk)[None, :]).astype(BF16)
    return pl.pallas_call(
        _attn_kernel,
        grid=(bsz, steps, nq),
        in_specs=[
            pl.BlockSpec((tq, w), lambda b, p, i: (b * nq + i, p)),
            pl.BlockSpec((seqlen, w), lambda b, p, i: (b, steps + p)),
            pl.BlockSpec((seqlen, w), lambda b, p, i: (b, 2 * steps + p)),
            pl.BlockSpec((tk, tk), lambda b, p, i: (0, 0)),
        ],
        out_specs=pl.BlockSpec((tq, w), lambda b, p, i: (b * nq + i, p)),
        out_shape=jax.ShapeDtypeStruct((bsz * seqlen, SB_WIDTH), BF16),
        scratch_shapes=[
            pltpu.VMEM((2, 2 * ATTN_PAIRS, tq, tk), F32),
            pltpu.VMEM((2, 2 * ATTN_PAIRS, tq, tk), BF16),
            pltpu.VMEM((ATTN_PAIRS, tq, LANES), F32),
        ],
        compiler_params=_compiler_params(("parallel", "parallel", "arbitrary")),
        name="sb_attention",
    )(qkv, qkv, qkv, u)


def _merge_kernel(x_ref, pssd_ref, ysb_ref, npre_ref, wg_ref, bg_ref, wsb_ref, wout_ref,
                  nw_ref, o_ref):
    for r in range(MERGE_ROW_BLOCKS):
        rows = slice(r * MERGE_TM // MERGE_ROW_BLOCKS, (r + 1) * MERGE_TM // MERGE_ROW_BLOCKS)
        x = x_ref[rows, :]
        n1 = (x * _rms_scale(x) * npre_ref[...]).astype(BF16)
        gates = _sigmoid(jnp.dot(n1, wg_ref[...], preferred_element_type=F32) + bg_ref[...])
        merged = (gates[:, 0:D_MODEL] * pssd_ref[rows, :]
                  + gates[:, D_MODEL:2 * D_MODEL]
                  * jnp.dot(ysb_ref[rows, :], wsb_ref[...], preferred_element_type=F32))
        o = jnp.dot(merged.astype(BF16), wout_ref[...], preferred_element_type=F32)
        o_ref[rows, :] = x + o * _rms_scale(o) * nw_ref[...]


def _merge(x, p_ssd, y_sb, norm_pre, w_gate, b_gate, w_sb, w_out, norm_w):
    t, d = x.shape
    tm = MERGE_TM
    const = lambda i: (0, 0)
    return pl.pallas_call(
        _merge_kernel,
        grid=(t // tm,),
        in_specs=[
            pl.BlockSpec((tm, d), lambda i: (i, 0)),
            pl.BlockSpec((tm, d), lambda i: (i, 0)),
            pl.BlockSpec((tm, SB_WIDTH), lambda i: (i, 0)),
            pl.BlockSpec((1, d), const),
            pl.BlockSpec((d, 2 * d), const),
            pl.BlockSpec((1, 2 * d), const),
            pl.BlockSpec((SB_WIDTH, d), const),
            pl.BlockSpec((d, d), const),
            pl.BlockSpec((1, d), const),
        ],
        out_specs=pl.BlockSpec((tm, d), lambda i: (i, 0)),
        out_shape=jax.ShapeDtypeStruct((t, d), F32),
        compiler_params=_compiler_params(("parallel",)),
        name="merge",
    )(x, p_ssd, y_sb, norm_pre, w_gate, b_gate.reshape(1, -1), w_sb, w_out, norm_w.reshape(1, -1))


def _ffn_kernel(h_ref, p_ref, npre_ref, w1_ref, w2_ref, npost_ref, wple_ref, wpg_ref, nple_ref,
                o_ref, n_ref, acc_ref):
    f = pl.program_id(1)

    @pl.when(f == 0)
    def _():
        h = h_ref[...]
        n_ref[...] = (h * _rms_scale(h) * npre_ref[...]).astype(BF16)
        acc_ref[...] = jnp.zeros_like(acc_ref)

    row_blocks = [slice(r * FFN_TM // FFN_ROW_BLOCKS, (r + 1) * FFN_TM // FFN_ROW_BLOCKS)
                  for r in range(FFN_ROW_BLOCKS)]
    for rows in row_blocks:
        a = jnp.maximum(jnp.dot(n_ref[rows, :], w1_ref[...], preferred_element_type=F32), 0.0)
        acc_ref[rows, :] += jnp.dot((a * a).astype(BF16), w2_ref[...], preferred_element_type=F32)

    @pl.when(f == pl.num_programs(1) - 1)
    def _():
        for rows in row_blocks:
            ff = acc_ref[rows, :]
            h2 = h_ref[rows, :] + ff * _rms_scale(ff) * npost_ref[...]
            gate = _sigmoid(jnp.dot(h2.astype(BF16), wpg_ref[...], preferred_element_type=F32))
            pe = gate * jnp.dot(p_ref[rows, :].astype(BF16), wple_ref[...],
                                preferred_element_type=F32)
            o_ref[rows, :] = h2 + pe * _rms_scale(pe) * nple_ref[...]


def _ffn(h, p, n_pre, w1, w2, n_post, w_ple, w_pg, n_ple):
    t, d = h.shape
    tm, tf = FFN_TM, FFN_TF
    const = lambda i, f: (0, 0)
    return pl.pallas_call(
        _ffn_kernel,
        grid=(t // tm, D_FF // tf),
        in_specs=[
            pl.BlockSpec((tm, d), lambda i, f: (i, 0)),
            pl.BlockSpec((tm, PLE_DIM), lambda i, f: (i, 0)),
            pl.BlockSpec((1, d), const),
            pl.BlockSpec((d, tf), lambda i, f: (0, f)),
            pl.BlockSpec((tf, d), lambda i, f: (f, 0)),
            pl.BlockSpec((1, d), const),
            pl.BlockSpec((PLE_DIM, d), const),
            pl.BlockSpec((d, d), const),
            pl.BlockSpec((1, d), const),
        ],
        out_specs=pl.BlockSpec((tm, d), lambda i, f: (i, 0)),
        out_shape=jax.ShapeDtypeStruct((t, d), F32),
        scratch_shapes=[pltpu.VMEM((tm, d), BF16), pltpu.VMEM((tm, d), F32)],
        compiler_params=_compiler_params(("parallel", "arbitrary")),
        name="ffn_ple",
    )(h, p, n_pre.reshape(1, -1), w1, w2, n_post.reshape(1, -1), w_ple, w_pg, n_ple.reshape(1, -1))


def _layer(h, p, norm_mix_pre, w_in, conv_w, conv_b, dt_bias, a_log, d_skip, ssd_norm,
           w_ssd_branch, w_sb_branch, w_gate, b_gate, w_out, norm_mix_post,
           norm_ffn_pre, w_ff1, w_ff2, norm_ffn_post, w_ple, w_ple_gate, norm_ple_post):
    bsz, seqlen, d = h.shape
    t = bsz * seqlen
    x2 = h.reshape(t, d)

    c_dt = D_INNER + D_INNER + 2 * SSD_GROUPS * D_STATE
    c_q = c_dt + SSD_HEADS
    w_in_b = w_in.astype(BF16)
    w_dt = jnp.pad(w_in_b[:, c_dt:c_q], ((0, 0), (0, LANES - SSD_HEADS)))
    q_scale = LOG2E * SB_HEAD_DIM ** -0.5
    w_qkv = jnp.concatenate([(w_in[:, c_q:c_q + SB_WIDTH] * q_scale).astype(BF16),
                             w_in_b[:, c_q + SB_WIDTH:]], axis=1)
    g_pre = norm_mix_pre.reshape(1, d)

    qkv = _norm_matmul(x2, g_pre, w_qkv, BF16, PROJ_TM, QKV_TN)

    p_ssd = _ssd(x2, g_pre, w_in_b, w_dt, conv_w, conv_b, dt_bias, a_log, d_skip, ssd_norm,
                 w_ssd_branch.astype(BF16), bsz, seqlen)
    y_sb = _attention(qkv, bsz, seqlen)

    h1 = _merge(x2, p_ssd, y_sb, g_pre, w_gate.astype(BF16), b_gate, w_sb_branch.astype(BF16),
                w_out.astype(BF16), norm_mix_post)
    h3 = _ffn(h1, p.reshape(t, PLE_DIM), norm_ffn_pre, w_ff1.astype(BF16), w_ff2.astype(BF16),
              norm_ffn_post, w_ple.astype(BF16), w_ple_gate.astype(BF16), norm_ple_post)
    return h3.reshape(bsz, seqlen, d)


def kernel(x, p, norm_mix_pre, w_in, conv_w, conv_b, dt_bias, a_log, d_skip, ssd_norm, w_ssd_branch, w_sb_branch, w_gate, b_gate, w_out, norm_mix_post, norm_ffn_pre, w_ff1, w_ff2, norm_ffn_post, w_ple, w_ple_gate, norm_ple_post):
    h = x
    for i in range(p.shape[0]):
        h = _layer(h, p[i], norm_mix_pre[i], w_in[i], conv_w[i], conv_b[i], dt_bias[i], a_log[i],
                   d_skip[i], ssd_norm[i], w_ssd_branch[i], w_sb_branch[i], w_gate[i], b_gate[i],
                   w_out[i], norm_mix_post[i], norm_ffn_pre[i], w_ff1[i], w_ff2[i],
                   norm_ffn_post[i], w_ple[i], w_ple_gate[i], norm_ple_post[i])
    return h
```

```python
import math

import jax
import jax.numpy as jnp
from jax import lax
from jax.experimental import pallas as pl
from jax.experimental.pallas import tpu as pltpu

F32 = jnp.float32
BF16 = jnp.bfloat16

RMS_EPS = 1e-6
LOG2E = math.log2(math.e)

D_MODEL = 1024
D_INNER = 2048
SSD_HEAD_DIM = 64
SSD_HEADS = 32
SSD_GROUPS = 8
HEADS_PER_GROUP = SSD_HEADS // SSD_GROUPS
D_STATE = 128
CONV_K = 4
SSD_CHUNK = 128
SB_HEADS = 16
SB_HEAD_DIM = 64
SB_WIDTH = 1024
D_FF = 4096
PLE_DIM = 256

LANES = 128
SUBLANES = 8
VMEM_LIMIT_BYTES = 56 * 1024 * 1024

PROJ_TM = 1024
QKV_TN = 1536
SSD_STEP_CHUNKS = 2
ATTN_TQ = 256
ATTN_TK = 256
ATTN_PAIRS = 4
MASKED_LOGIT = -1e30
MERGE_TM = 1024
MERGE_ROW_BLOCKS = 4
FFN_TM = 1024
FFN_TF = 1024
FFN_ROW_BLOCKS = 2
RESIDENT = pl.Buffered(1)


def _compiler_params(semantics):
    return pltpu.CompilerParams(dimension_semantics=semantics, vmem_limit_bytes=VMEM_LIMIT_BYTES)


def _rms_scale(x):
    return lax.rsqrt(jnp.mean(x * x, axis=-1, keepdims=True) + RMS_EPS)


def _sigmoid(x):
    return 1.0 / (1.0 + jnp.exp(-x))


def _silu(x):
    h = 0.5 * x
    return h + h * jnp.tanh(h)


def _neg_abs(x):
    bits = lax.bitcast_convert_type(x, jnp.int32) | jnp.int32(-2 ** 31)
    return lax.bitcast_convert_type(bits, F32)


def _softplus(x):
    return jnp.maximum(x, 0.0) + jnp.log(1.0 + jnp.exp(-jnp.abs(x)))


def _norm_matmul_kernel(x_ref, g_ref, w_ref, o_ref, n_ref):
    @pl.when(pl.program_id(1) == 0)
    def _():
        x = x_ref[...]
        n_ref[...] = (x * _rms_scale(x) * g_ref[...]).astype(BF16)

    o_ref[...] = jnp.dot(n_ref[...], w_ref[...], preferred_element_type=F32).astype(o_ref.dtype)


def _norm_matmul(x, g, w, out_dtype, tm, tn):
    t, d = x.shape
    n = w.shape[1]
    return pl.pallas_call(
        _norm_matmul_kernel,
        grid=(t // tm, n // tn),
        in_specs=[
            pl.BlockSpec((tm, d), lambda i, j: (i, 0)),
            pl.BlockSpec((1, d), lambda i, j: (0, 0)),
            pl.BlockSpec((d, tn), lambda i, j: (0, j)),
        ],
        out_specs=pl.BlockSpec((tm, tn), lambda i, j: (i, j)),
        out_shape=jax.ShapeDtypeStruct((t, n), out_dtype),
        scratch_shapes=[pltpu.VMEM((tm, d), BF16)],
        compiler_params=_compiler_params(("parallel", "arbitrary")),
        name="norm_matmul",
    )(x, g, w)


def _split3(x):
    p1 = x.astype(BF16)
    r1 = x - p1.astype(F32)
    p2 = r1.astype(BF16)
    r2 = r1 - p2.astype(F32)
    return p1, p2, r2.astype(BF16)


def _ssd_kernel(x_ref, g_ref, w_ref, wdt_ref, cw_ref, cb_ref,
                dtb_ref, alog_ref, dskip_ref, normw_ref, wout_ref, o_ref,
                tail_ref, state_ref, y_ref):
    c = pl.program_id(1)
    last = pl.num_programs(1) - 1

    def project_previous():
        o_ref[...] = jnp.dot(y_ref[...], wout_ref[...], preferred_element_type=F32)

    @pl.when(c == 0)
    def _():
        y_ref[...] = jnp.zeros_like(y_ref)
        tail_ref[...] = jnp.zeros_like(tail_ref)
        state_ref[...] = jnp.zeros_like(state_ref)

    @pl.when(c < last)
    def _():
        _ssd_chunks(project_previous, x_ref, g_ref, w_ref, wdt_ref, cw_ref, cb_ref, dtb_ref,
                    alog_ref, dskip_ref, normw_ref, tail_ref, state_ref, y_ref)

    @pl.when(c == last)
    def _():
        project_previous()


def _ssd_chunks(project_previous, x_ref, g_ref, w_ref, wdt_ref, cw_ref, cb_ref, dtb_ref,
                alog_ref, dskip_ref, normw_ref, tail_ref, state_ref, y_ref):
    L = SSD_CHUNK
    rows_per_step = SSD_STEP_CHUNKS * L
    x = x_ref[...]
    n1 = (x * _rms_scale(x) * g_ref[...]).astype(BF16)

    def project(lo, width):
        return jnp.dot(n1, w_ref[:, lo:lo + width], preferred_element_type=F32)

    dt_raw = jnp.dot(n1, wdt_ref[...], preferred_element_type=F32)
    dt_all = _softplus(dt_raw + dtb_ref[...])
    a = -jnp.exp(alog_ref[...])
    row = lax.broadcasted_iota(jnp.int32, (L, L), 0)
    col = lax.broadcasted_iota(jnp.int32, (L, L), 1)
    causal = row >= col
    tril = jnp.where(causal, 1.0, 0.0).astype(BF16)
    first_half = lax.broadcasted_iota(jnp.int32, (L, LANES), 1) < SSD_HEAD_DIM

    chunks = []
    for ck in range(SSD_STEP_CHUNKS):
        dt = dt_all[ck * L:(ck + 1) * L, :]
        p1, p2, p3 = _split3(dt * a)
        a_cs = (jnp.dot(tril, p1, preferred_element_type=F32)
                + jnp.dot(tril, p2, preferred_element_type=F32)
                + jnp.dot(tril, p3, preferred_element_type=F32))
        a_tot = a_cs[L - 1:L, :]
        w_end = dt * jnp.exp(a_tot - a_cs)
        chunks.append((a_cs.T, (a_cs - jnp.log(dt)).T, w_end.T, jnp.exp(a_cs).T))
    project_previous()

    def conv_silu(lo, width):
        cols = slice(lo, lo + width)
        raw = project(D_INNER + lo, width)
        ext = jnp.concatenate([tail_ref[:, cols], raw], axis=0)
        tail_ref[:, cols] = raw[rows_per_step - SUBLANES:, :]
        acc = cw_ref[0:1, cols] * ext
        for k in range(1, CONV_K):
            acc = cw_ref[k:k + 1, cols] * ext + pltpu.roll(acc, 1, 0)
        acc = acc[SUBLANES:, :] + cb_ref[:, cols]
        return _silu(acc)

    def col_bcast(mat_t, h):
        return jnp.broadcast_to(mat_t[h:h + 1, :], (L, L)).T

    def pair_bcast(mat_t, h):
        halves = [jnp.broadcast_to(mat_t[h + k:h + k + 1, :], (SSD_HEAD_DIM, L)) for k in range(2)]
        return jnp.concatenate(halves, axis=0).T

    def group_bcast(mat_t, g):
        h = g * HEADS_PER_GROUP
        return jnp.concatenate([pair_bcast(mat_t, h), pair_bcast(mat_t, h + 2)], axis=1)

    x_groups, b_pairs, c_pairs, z_gates = [], [], [], []
    for g in range(SSD_GROUPS):
        x_groups.append(conv_silu(g * 256, 256))
        if g % 2 == 0:
            b_pairs.append(conv_silu(D_INNER + g * D_STATE, 2 * D_STATE).astype(BF16))
            c_pairs.append(conv_silu(D_INNER + (SSD_GROUPS + g) * D_STATE, 2 * D_STATE).astype(BF16))
        z = project(g * 256, 256)
        z_gates.append(_silu(z))

    for g in range(SSD_GROUPS):
        gcols = slice(g * 256, (g + 1) * 256)
        x_all, b_pair, c_pair, z_all = x_groups[g], b_pairs[g // 2], c_pairs[g // 2], z_gates[g]
        half = slice((g % 2) * D_STATE, (g % 2 + 1) * D_STATE)

        for ck, (a_cs_t, a_cs_dt_t, w_end_t, ea_t) in enumerate(chunks):
            rows = slice(ck * L, (ck + 1) * L)
            xg = x_all[rows, :]
            xg_b = xg.astype(BF16)
            bg_b = b_pair[rows, half]
            cg_b = c_pair[rows, half]
            cb = lax.dot_general(cg_b, bg_b, (((1,), (1,)), ((), ())),
                                 preferred_element_type=F32)
            st = state_ref[g]
            y_off = jnp.dot(cg_b, st.astype(BF16), preferred_element_type=F32)

            y_pairs = []
            for jp in range(HEADS_PER_GROUP // 2):
                x_pair = xg_b[:, jp * LANES:(jp + 1) * LANES]
                zero = jnp.zeros_like(x_pair)
                x_halves = (jnp.where(first_half, x_pair, zero), jnp.where(first_half, zero, x_pair))
                parts = []
                for k in range(2):
                    h = g * HEADS_PER_GROUP + 2 * jp + k
                    seg = col_bcast(a_cs_t, h) - a_cs_dt_t[h:h + 1, :]
                    m = (cb * jnp.exp(jnp.where(causal, seg, -jnp.inf))).astype(BF16)
                    parts.append(jnp.dot(m, x_halves[k], preferred_element_type=F32))
                y_pairs.append(parts[0] + parts[1])
            yd = jnp.concatenate(y_pairs, axis=1)
            ea = group_bcast(ea_t, g)
            wf = group_bcast(w_end_t, g)
            y = yd + y_off * ea + xg * dskip_ref[:, gcols]

            upd = lax.dot_general(bg_b, (xg * wf).astype(BF16), (((0,), (0,)), ((), ())),
                                  preferred_element_type=F32)
            state_ref[g] = st * ea[L - 1:L, :] + upd

            y = y * z_all[rows, :]
            y = y * _rms_scale(y) * normw_ref[:, gcols]
            y_ref[rows, gcols] = y.astype(y_ref.dtype)


def _ssd(x, g_pre, w_in, w_dt, conv_w, conv_b, dt_bias, a_log, d_skip, ssd_norm, w_branch, bsz, seqlen):
    L = SSD_STEP_CHUNKS * SSD_CHUNK
    nc = seqlen // L
    steps = nc + 1
    d = x.shape[1]
    pad = LANES - SSD_HEADS
    dtb = jnp.pad(dt_bias, (0, pad)).reshape(1, LANES)
    alog = jnp.pad(a_log, (0, pad)).reshape(1, LANES)
    dskip = jnp.repeat(d_skip, SSD_HEAD_DIM).reshape(1, D_INNER)
    rows = lambda b, c: b * nc + jnp.minimum(c, nc - 1)
    out_rows = lambda b, c: b * nc + jnp.maximum(c - 1, 0)
    const = lambda b, c: (0, 0)
    conv_dim = 2 * D_INNER
    return pl.pallas_call(
        _ssd_kernel,
        grid=(bsz, steps),
        in_specs=[
            pl.BlockSpec((L, d), lambda b, c: (rows(b, c), 0)),
            pl.BlockSpec((1, d), const),
            pl.BlockSpec((d, D_INNER + conv_dim), const),
            pl.BlockSpec((d, LANES), const),
            pl.BlockSpec((CONV_K, conv_dim), const),
            pl.BlockSpec((1, conv_dim), const),
            pl.BlockSpec((1, LANES), const),
            pl.BlockSpec((1, LANES), const),
            pl.BlockSpec((1, D_INNER), const),
            pl.BlockSpec((1, D_INNER), const),
            pl.BlockSpec((D_INNER, D_MODEL), const),
        ],
        out_specs=pl.BlockSpec((L, D_MODEL), lambda b, c: (out_rows(b, c), 0)),
        out_shape=jax.ShapeDtypeStruct((bsz * seqlen, D_MODEL), F32),
        scratch_shapes=[
            pltpu.VMEM((SUBLANES, conv_dim), F32),
            pltpu.VMEM((SSD_GROUPS, D_STATE, HEADS_PER_GROUP * SSD_HEAD_DIM), F32),
            pltpu.VMEM((L, D_INNER), BF16),
        ],
        compiler_params=_compiler_params(("parallel", "arbitrary")),
        name="ssd",
    )(x, g_pre, w_in, w_dt, conv_w, conv_b.reshape(1, -1), dtb, alog, dskip,
      ssd_norm.reshape(1, D_INNER), w_branch)


def _attn_kernel(q_ref, k_ref, v_ref, u_ref, o_ref, z_scr, sp_scr, acc_scr):
    tq, tk = ATTN_TQ, ATTN_TK
    n_heads = 2 * ATTN_PAIRS
    qi = pl.program_id(2)
    first_half = lax.broadcasted_iota(jnp.int32, (tq, LANES), 1) < SB_HEAD_DIM
    q_heads = []
    for p in range(ATTN_PAIRS):
        q2 = q_ref[:, p * LANES:(p + 1) * LANES]
        zero = jnp.zeros_like(q2)
        q_heads += [jnp.where(first_half, q2, zero), jnp.where(first_half, zero, q2)]
    u = u_ref[...]
    strictly_below = (lax.broadcasted_iota(jnp.int32, (tq, tk), 1)
                      < lax.broadcasted_iota(jnp.int32, (tq, tk), 0))

    def logits_head(kb, slot, h, later, masked):
        start = pl.multiple_of(kb * tk, tk)
        p = h // 2
        kt = k_ref[pl.ds(start, tk), p * LANES:(p + 1) * LANES]
        z = lax.dot_general(q_heads[h], kt, (((1,), (1,)), ((), ())),
                            preferred_element_type=F32)
        sp = jnp.maximum(z, 0.0) + jnp.log2(1.0 + jnp.exp2(_neg_abs(z)))
        if masked:
            sp = jnp.where(strictly_below, sp, 0.0)
            z = jnp.where(strictly_below, z, MASKED_LOGIT)
        own = z - sp
        z_scr[slot, h] = own if later is None else own - later
        sp_scr[slot, h] = sp.astype(BF16)
        sums = jnp.sum(sp, axis=1, keepdims=True)
        return sums if later is None else later + sums

    def suffix_head(slot, h):
        return jnp.dot(sp_scr[slot, h], u, preferred_element_type=F32)

    def value_head(kb, slot, h, csum):
        start = pl.multiple_of(kb * tk, tk)
        p = h // 2
        a = jnp.exp2(z_scr[slot, h] - csum)
        vt = v_ref[pl.ds(start, tk), p * LANES:(p + 1) * LANES]
        return jnp.dot(a.astype(BF16), vt, preferred_element_type=F32)

    def accumulate(p, out_even, out_odd, first):
        pair = jnp.where(first_half, out_even, out_odd)
        acc_scr[p] = pair if first else acc_scr[p] + pair

    def first_step(slot):
        return tuple(logits_head(qi, slot, h, None, True) for h in range(n_heads))

    def step(kb, old_slot, new_slot, later, first=False):
        new_later, csums, outs = [], [], []
        for t in range(n_heads + 1):
            if t < n_heads:
                csums.append(suffix_head(old_slot, t))
                new_later.append(logits_head(kb, new_slot, t, later[t], False))
            if t >= 1:
                outs.append(value_head(kb + 1, old_slot, t - 1, csums[t - 1]))
                if t % 2 == 0:
                    accumulate(t // 2 - 1, outs[t - 2], outs[t - 1], first)
        return tuple(new_later)

    def last_step(slot, first):
        csums = [suffix_head(slot, h) for h in range(n_heads)]
        for p in range(ATTN_PAIRS):
            accumulate(p, value_head(0, slot, 2 * p, csums[2 * p]),
                       value_head(0, slot, 2 * p + 1, csums[2 * p + 1]), first)

    odd = lax.rem(qi, 2)

    @pl.when(qi == 0)
    def _():
        first_step(0)
        last_step(0, True)

    @pl.when(qi > 0)
    def _():
        def odd_start():
            return step(qi - 1, 1, 0, first_step(1), first=True)

        def even_start():
            later = first_step(0)
            later = step(qi - 1, 0, 1, later, first=True)
            return step(qi - 2, 1, 0, later)

        later = lax.cond(odd == 1, odd_start, even_start)
        remaining = qi - 2 + odd

        def body(j, later):
            kb = remaining - 1 - 2 * j
            return step(kb - 1, 1, 0, step(kb, 0, 1, later))

        lax.fori_loop(0, remaining // 2, body, later)
        last_step(0, False)

    for p in range(ATTN_PAIRS):
        o_ref[:, p * LANES:(p + 1) * LANES] = acc_scr[p].astype(o_ref.dtype)


def _attention(qkv, bsz, seqlen):
    tq, tk = ATTN_TQ, ATTN_TK
    nq = seqlen // tq
    w = ATTN_PAIRS * LANES
    steps = SB_WIDTH // w
    u = (jnp.arange(tk)[:, None] > jnp.arange(tk)[None, :]).astype(BF16)
    return pl.pallas_call(
        _attn_kernel,
        grid=(bsz, steps, nq),
        in_specs=[
            pl.BlockSpec((tq, w), lambda b, p, i: (b * nq + i, p)),
            pl.BlockSpec((seqlen, w), lambda b, p, i: (b, steps + p)),
            pl.BlockSpec((seqlen, w), lambda b, p, i: (b, 2 * steps + p)),
            pl.BlockSpec((tk, tk), lambda b, p, i: (0, 0)),
        ],
        out_specs=pl.BlockSpec((tq, w), lambda b, p, i: (b * nq + i, p)),
        out_shape=jax.ShapeDtypeStruct((bsz * seqlen, SB_WIDTH), BF16),
        scratch_shapes=[
            pltpu.VMEM((2, 2 * ATTN_PAIRS, tq, tk), F32),
            pltpu.VMEM((2, 2 * ATTN_PAIRS, tq, tk), BF16),
            pltpu.VMEM((ATTN_PAIRS, tq, LANES), F32),
        ],
        compiler_params=_compiler_params(("parallel", "parallel", "arbitrary")),
        name="sb_attention",
    )(qkv, qkv, qkv, u)


def _merge_kernel(x_ref, pssd_ref, ysb_ref, npre_ref, wg_ref, bg_ref, wsb_ref, wout_ref,
                  nw_ref, o_ref):
    for r in range(MERGE_ROW_BLOCKS):
        rows = slice(r * MERGE_TM // MERGE_ROW_BLOCKS, (r + 1) * MERGE_TM // MERGE_ROW_BLOCKS)
        x = x_ref[rows, :]
        n1 = (x * _rms_scale(x) * npre_ref[...]).astype(BF16)
        gates = _sigmoid(jnp.dot(n1, wg_ref[...], preferred_element_type=F32) + bg_ref[...])
        merged = (gates[:, 0:D_MODEL] * pssd_ref[rows, :]
                  + gates[:, D_MODEL:2 * D_MODEL]
                  * jnp.dot(ysb_ref[rows, :], wsb_ref[...], preferred_element_type=F32))
        o = jnp.dot(merged.astype(BF16), wout_ref[...], preferred_element_type=F32)
        o_ref[rows, :] = x + o * _rms_scale(o) * nw_ref[...]


def _merge(x, p_ssd, y_sb, norm_pre, w_gate, b_gate, w_sb, w_out, norm_w):
    t, d = x.shape
    tm = MERGE_TM
    const = lambda i: (0, 0)
    return pl.pallas_call(
        _merge_kernel,
        grid=(t // tm,),
        in_specs=[
            pl.BlockSpec((tm, d), lambda i: (i, 0)),
            pl.BlockSpec((tm, d), lambda i: (i, 0)),
            pl.BlockSpec((tm, SB_WIDTH), lambda i: (i, 0)),
            pl.BlockSpec((1, d), const, pipeline_mode=RESIDENT),
            pl.BlockSpec((d, 2 * d), const, pipeline_mode=RESIDENT),
            pl.BlockSpec((1, 2 * d), const, pipeline_mode=RESIDENT),
            pl.BlockSpec((SB_WIDTH, d), const, pipeline_mode=RESIDENT),
            pl.BlockSpec((d, d), const, pipeline_mode=RESIDENT),
            pl.BlockSpec((1, d), const, pipeline_mode=RESIDENT),
        ],
        out_specs=pl.BlockSpec((tm, d), lambda i: (i, 0)),
        out_shape=jax.ShapeDtypeStruct((t, d), F32),
        compiler_params=_compiler_params(("parallel",)),
        name="merge",
    )(x, p_ssd, y_sb, norm_pre, w_gate, b_gate.reshape(1, -1), w_sb, w_out, norm_w.reshape(1, -1))


def _ffn_kernel(h_ref, p_ref, npre_ref, w1_ref, w2_ref, npost_ref, wple_ref, wpg_ref, nple_ref,
                o_ref, n_ref, acc_ref):
    f = pl.program_id(1)

    @pl.when(f == 0)
    def _():
        h = h_ref[...]
        n_ref[...] = (h * _rms_scale(h) * npre_ref[...]).astype(BF16)
        acc_ref[...] = jnp.zeros_like(acc_ref)

    row_blocks = [slice(r * FFN_TM // FFN_ROW_BLOCKS, (r + 1) * FFN_TM // FFN_ROW_BLOCKS)
                  for r in range(FFN_ROW_BLOCKS)]
    for rows in row_blocks:
        a = jnp.maximum(jnp.dot(n_ref[rows, :], w1_ref[...], preferred_element_type=F32), 0.0)
        acc_ref[rows, :] += jnp.dot((a * a).astype(BF16), w2_ref[...], preferred_element_type=F32)

    @pl.when(f == pl.num_programs(1) - 1)
    def _():
        for rows in row_blocks:
            ff = acc_ref[rows, :]
            h2 = h_ref[rows, :] + ff * _rms_scale(ff) * npost_ref[...]
            gate = _sigmoid(jnp.dot(h2.astype(BF16), wpg_ref[...], preferred_element_type=F32))
            pe = gate * jnp.dot(p_ref[rows, :].astype(BF16), wple_ref[...],
                                preferred_element_type=F32)
            o_ref[rows, :] = h2 + pe * _rms_scale(pe) * nple_ref[...]


def _ffn(h, p, n_pre, w1, w2, n_post, w_ple, w_pg, n_ple):
    t, d = h.shape
    tm, tf = FFN_TM, FFN_TF
    const = lambda i, f: (0, 0)
    return pl.pallas_call(
        _ffn_kernel,
        grid=(t // tm, D_FF // tf),
        in_specs=[
            pl.BlockSpec((tm, d), lambda i, f: (i, 0)),
            pl.BlockSpec((tm, PLE_DIM), lambda i, f: (i, 0)),
            pl.BlockSpec((1, d), const, pipeline_mode=RESIDENT),
            pl.BlockSpec((d, tf), lambda i, f: (0, f)),
            pl.BlockSpec((tf, d), lambda i, f: (f, 0)),
            pl.BlockSpec((1, d), const, pipeline_mode=RESIDENT),
            pl.BlockSpec((PLE_DIM, d), const, pipeline_mode=RESIDENT),
            pl.BlockSpec((d, d), const, pipeline_mode=RESIDENT),
            pl.BlockSpec((1, d), const, pipeline_mode=RESIDENT),
        ],
        out_specs=pl.BlockSpec((tm, d), lambda i, f: (i, 0)),
        out_shape=jax.ShapeDtypeStruct((t, d), F32),
        scratch_shapes=[pltpu.VMEM((tm, d), BF16), pltpu.VMEM((tm, d), F32)],
        compiler_params=_compiler_params(("parallel", "arbitrary")),
        name="ffn_ple",
    )(h, p, n_pre.reshape(1, -1), w1, w2, n_post.reshape(1, -1), w_ple, w_pg, n_ple.reshape(1, -1))


def _layer(h, p, norm_mix_pre, w_in, conv_w, conv_b, dt_bias, a_log, d_skip, ssd_norm,
           w_ssd_branch, w_sb_branch, w_gate, b_gate, w_out, norm_mix_post,
           norm_ffn_pre, w_ff1, w_ff2, norm_ffn_post, w_ple, w_ple_gate, norm_ple_post):
    bsz, seqlen, d = h.shape
    t = bsz * seqlen
    x2 = h.reshape(t, d)

    c_dt = D_INNER + D_INNER + 2 * SSD_GROUPS * D_STATE
    c_q = c_dt + SSD_HEADS
    w_in_b = w_in.astype(BF16)
    w_dt = jnp.pad(w_in_b[:, c_dt:c_q], ((0, 0), (0, LANES - SSD_HEADS)))
    q_scale = LOG2E * SB_HEAD_DIM ** -0.5
    w_qkv = jnp.concatenate([(w_in[:, c_q:c_q + SB_WIDTH] * q_scale).astype(BF16),
                             w_in_b[:, c_q + SB_WIDTH:]], axis=1)
    g_pre = norm_mix_pre.reshape(1, d)

    qkv = _norm_matmul(x2, g_pre, w_qkv, BF16, PROJ_TM, QKV_TN)

    p_ssd = _ssd(x2, g_pre, w_in_b, w_dt, conv_w, conv_b, dt_bias, a_log, d_skip, ssd_norm,
                 w_ssd_branch.astype(BF16), bsz, seqlen)
    y_sb = _attention(qkv, bsz, seqlen)

    h1 = _merge(x2, p_ssd, y_sb, g_pre, w_gate.astype(BF16), b_gate, w_sb_branch.astype(BF16),
                w_out.astype(BF16), norm_mix_post)
    h3 = _ffn(h1, p.reshape(t, PLE_DIM), norm_ffn_pre, w_ff1.astype(BF16), w_ff2.astype(BF16),
              norm_ffn_post, w_ple.astype(BF16), w_ple_gate.astype(BF16), norm_ple_post)
    return h3.reshape(bsz, seqlen, d)


def kernel(x, p, norm_mix_pre, w_in, conv_w, conv_b, dt_bias, a_log, d_skip, ssd_norm, w_ssd_branch, w_sb_branch, w_gate, b_gate, w_out, norm_mix_post, norm_ffn_pre, w_ff1, w_ff2, norm_ffn_post, w_ple, w_ple_gate, norm_ple_post):
    h = x
    for i in range(p.shape[0]):
        h = _layer(h, p[i], norm_mix_pre[i], w_in[i], conv_w[i], conv_b[i], dt_bias[i], a_log[i],
                   d_skip[i], ssd_norm[i], w_ssd_branch[i], w_sb_branch[i], w_gate[i], b_gate[i],
                   w_out[i], norm_mix_post[i], norm_ffn_pre[i], w_ff1[i], w_ff2[i],
                   norm_ffn_post[i], w_ple[i], w_ple_gate[i], norm_ple_post[i])
    return h
```
